```python
import math
import jax, jax.numpy as jnp
from jax import lax
import numpy as np

D_MODEL = 1024
BATCH = 4
SEQ = 4096
DEPTH = 2
DEC_BATCH = 8
DEC_SEQ = 64
PAST_LEN = 1024

CHUNK = 64
DA_HEADS = 4
DA_HEAD_DIM = 64
DA_VDIM = 2 * DA_HEAD_DIM
DA_WIDTH = DA_HEADS * DA_VDIM
ML_HEADS = 4
ML_HEAD_DIM = 64
ML_WIDTH = ML_HEADS * ML_HEAD_DIM
ML_CONV = 4
CM_GROUPS = 4
CM_WIDTH = 256
CM_GROUP_DIM = CM_WIDTH // CM_GROUPS
CM_CHUNK = 128
MIX_WIDTH = DA_WIDTH + ML_WIDTH + CM_WIDTH
SPLIT_SIZES = (DA_WIDTH, DA_WIDTH, DA_WIDTH, ML_WIDTH, ML_WIDTH, ML_WIDTH, ML_HEADS, ML_HEADS, CM_WIDTH, CM_WIDTH)
IN_WIDTH = sum(SPLIT_SIZES)
PEER_HEADS = 8
PEER_KEYS = 128
PEER_EXPERTS = PEER_KEYS * PEER_KEYS
PEER_QDIM = 256
PEER_HALF = PEER_QDIM // 2
PEER_TOPK = 16
PEER_BLOCK = 128
REL_BUCKETS = 32
REL_MAX_DIST = 128
QBLOCK = 128
EPS = 1e-6
NEG_INF = -1e30

kernel_name = "hybrid_stream_encoder_step"


def rmsnorm(x, g):
    xf = x.astype(jnp.float32)
    y = xf * lax.rsqrt(jnp.mean(xf * xf, axis=-1, keepdims=True) + EPS)
    return (y * g.astype(jnp.float32)).astype(x.dtype)


def split_proj(p):
    offs = [int(o) for o in np.cumsum(SPLIT_SIZES)[:-1]]
    return jnp.split(p, offs, axis=-1)


def rel_bucket(rel):
    half = REL_BUCKETS // 2
    max_exact = half // 2
    ret = jnp.where(rel > 0, half, 0)
    n = jnp.abs(rel)
    nf = jnp.maximum(n, 1).astype(jnp.float32)
    large = max_exact + (jnp.log(nf / max_exact) / math.log(REL_MAX_DIST / max_exact)
                         * (half - max_exact)).astype(jnp.int32)
    large = jnp.minimum(large, half - 1)
    return ret + jnp.where(n < max_exact, n, large)


def rel_bias(q_pos, k_pos, table):
    b = rel_bucket(k_pos[None, :] - q_pos[:, None])
    return jnp.transpose(table[b], (2, 0, 1)).astype(jnp.float32)


def diff_weights(q, k, bias, mask, lam):
    s = jnp.einsum('bqhcd,bkhcd->bhcqk', q, k).astype(jnp.float32) * (DA_HEAD_DIM ** -0.5)
    s = s + bias[None, :, None]
    if mask is not None:
        s = jnp.where(mask, s, NEG_INF)
    p = jax.nn.softmax(s, axis=-1)
    return p[:, :, 0] - lam * p[:, :, 1]


def diff_attention_prompt(q, k, v, rel_table, lam):
    B, S = q.shape[0], q.shape[1]
    nb = S // QBLOCK
    kk = k.reshape(B, S, DA_HEADS, 2, DA_HEAD_DIM)
    k_pos = jnp.arange(S)
    qb = jnp.moveaxis(q.reshape(B, nb, QBLOCK, DA_HEADS, 2, DA_HEAD_DIM), 1, 0)

    def block(args):
        qi, bi = args
        q_pos = bi * QBLOCK + jnp.arange(QBLOCK)
        mask = (k_pos[None, :] // CHUNK) <= (q_pos[:, None] // CHUNK)
        a = diff_weights(qi, kk, rel_bias(q_pos, k_pos, rel_table), mask, lam)
        return jnp.einsum('bhqk,bkhv->bqhv', a.astype(v.dtype), v)

    o = lax.map(block, (qb, jnp.arange(nb)))
    return jnp.moveaxis(o, 0, 1).reshape(B, S, DA_HEADS, DA_VDIM)


def diff_attention_step(q, k_all, v_all, rel_table, lam, past_len):
    B, L = q.shape[0], q.shape[1]
    K = k_all.shape[1]
    kk = k_all.reshape(B, K, DA_HEADS, 2, DA_HEAD_DIM)
    q_pos = past_len + jnp.arange(L)
    k_pos = jnp.arange(K)
    a = diff_weights(q, kk, rel_bias(q_pos, k_pos, rel_table), None, lam)
    return jnp.einsum('bhqk,bkhv->bqhv', a.astype(v_all.dtype), v_all)


def causal_conv(x, prev, w, b):
    L = x.shape[1]
    xp = jnp.concatenate([prev, x], axis=1)
    y = b
    for i in range(ML_CONV):
        y = y + xp[:, i:i + L] * w[i]
    return y, xp[:, xp.shape[1] - (ML_CONV - 1):]


def mlstm_chunk(state, inp):
    C, n, m = state
    q, k, v, ig, lf = inp
    L = q.shape[2]
    F = jnp.cumsum(lf, axis=-1)
    causal = jnp.tril(jnp.ones((L, L), dtype=bool))
    D = jnp.where(causal, F[..., :, None] - F[..., None, :] + ig[..., None, :], NEG_INF)
    inter = F + m[..., None]
    m_t = jnp.maximum(inter, jnp.max(D, axis=-1))
    Sw = jnp.einsum('bhtd,bhsd->bhts', q, k) * jnp.exp(D - m_t[..., None])
    iw = jnp.exp(inter - m_t)
    num = jnp.einsum('bhts,bhsv->bhtv', Sw, v) + iw[..., None] * jnp.einsum('bhtd,bhdv->bhtv', q, C)
    den = jnp.sum(Sw, axis=-1) + iw * jnp.einsum('bhtd,bhd->bht', q, n)
    h = num / jnp.maximum(jnp.abs(den), jnp.exp(-m_t))[..., None]
    FL = F[..., -1]
    tail = FL[..., None] - F + ig
    m_new = jnp.maximum(FL + m, jnp.max(tail, axis=-1))
    wc = jnp.exp(FL + m - m_new)
    ws = jnp.exp(tail - m_new[..., None])
    C_new = wc[..., None, None] * C + jnp.einsum('bhs,bhsd,bhsv->bhdv', ws, k, v)
    n_new = wc[..., None] * n + jnp.einsum('bhs,bhsd->bhd', ws, k)
    return (C_new, n_new, m_new), h


def peer(x2d, w_q, sub_keys, u_tab, v_tab):
    T = x2d.shape[0]
    Tp = ((T + PEER_BLOCK - 1) // PEER_BLOCK) * PEER_BLOCK
    xb = jnp.pad(x2d, ((0, Tp - T), (0, 0))).reshape(Tp // PEER_BLOCK, PEER_BLOCK, D_MODEL)
    keys = sub_keys.astype(jnp.float32)

    def block(xi):
        q = (xi @ w_q).astype(jnp.float32).reshape(PEER_BLOCK, PEER_HEADS, PEER_QDIM)
        q = q * lax.rsqrt(jnp.mean(q * q, axis=-1, keepdims=True) + EPS)
        q = q.reshape(PEER_BLOCK, PEER_HEADS, 2, PEER_HALF)
        s = jnp.einsum('thcd,hckd->thck', q, keys)
        v1, i1 = lax.top_k(s[:, :, 0], PEER_TOPK)
        v2, i2 = lax.top_k(s[:, :, 1], PEER_TOPK)
        cand = (v1[..., :, None] + v2[..., None, :]).reshape(PEER_BLOCK, PEER_HEADS, PEER_TOPK * PEER_TOPK)
        vs, ci = lax.top_k(cand, PEER_TOPK)
        e = (jnp.take_along_axis(i1, ci // PEER_TOPK, axis=-1) * PEER_KEYS
             + jnp.take_along_axis(i2, ci % PEER_TOPK, axis=-1))
        g = jax.nn.softmax(vs, axis=-1)
        act = jax.nn.gelu(jnp.einsum('thkd,td->thk', u_tab[e], xi).astype(jnp.float32), approximate=False)
        return jnp.einsum('thk,thkd->td', (g * act).astype(v_tab.dtype), v_tab[e])

    y = lax.map(block, xb).reshape(Tp, D_MODEL)
    return y[:T]


def layer(x, past, l, rel_table, norm1_g, w_in, da_lambda, da_subln_g, ml_conv_w, ml_conv_b,
          ml_wq, ml_wk, ml_gate_b, ml_norm_g, ml_skip, cm_norm_g, cm_ws, cm_b, w_out,
          norm2_g, peer_wq, peer_keys, peer_u, peer_v):
    B, L = x.shape[0], x.shape[1]
    f32 = jnp.float32
    lam_init = 0.8 - 0.6 * math.exp(-0.3 * l)
    xn = rmsnorm(x, norm1_g)
    qa, ka, va, mc, mv, mo, mi, mf, cu, cv = split_proj(xn @ w_in)

    q = qa.reshape(B, L, DA_HEADS, 2, DA_HEAD_DIM)
    k_rows = ka.reshape(B, L, DA_HEADS, 2 * DA_HEAD_DIM)
    v_rows = va.reshape(B, L, DA_HEADS, DA_VDIM)
    lp = da_lambda.astype(f32)
    lam = jnp.exp(jnp.sum(lp[0] * lp[1])) - jnp.exp(jnp.sum(lp[2] * lp[3])) + lam_init
    if past is None:
        o_da = diff_attention_prompt(q, k_rows, v_rows, rel_table, lam)
    else:
        past_k, past_v = past[0], past[1]
        k_all = jnp.concatenate([past_k, k_rows], axis=1)
        v_all = jnp.concatenate([past_v, v_rows], axis=1)
        o_da = diff_attention_step(q, k_all, v_all, rel_table, lam, past_k.shape[1])
    o_da = (rmsnorm(o_da, da_subln_g) * (1.0 - lam_init)).reshape(B, L, DA_WIDTH)

    prev = jnp.zeros((B, ML_CONV - 1, ML_WIDTH), x.dtype) if past is None else past[5]
    cc, conv_new = causal_conv(mc, prev, ml_conv_w, ml_conv_b)
    cc = jax.nn.silu(cc)
    cch = cc.reshape(B, L, ML_HEADS, ML_HEAD_DIM)
    qm = jnp.einsum('blhd,hde->bhle', cch, ml_wq).astype(f32)
    km = jnp.einsum('blhd,hde->bhle', cch, ml_wk).astype(f32) * (ML_HEAD_DIM ** -0.5)
    vm = jnp.transpose(mv.reshape(B, L, ML_HEADS, ML_HEAD_DIM), (0, 2, 1, 3)).astype(f32)
    ig = jnp.transpose((mi + ml_gate_b[0]).astype(f32), (0, 2, 1))
    lf = jax.nn.log_sigmoid(jnp.transpose((mf + ml_gate_b[1]).astype(f32), (0, 2, 1)))
    if past is None:
        nc = L // CHUNK
        state0 = (jnp.zeros((B, ML_HEADS, ML_HEAD_DIM, ML_HEAD_DIM), f32),
                  jnp.zeros((B, ML_HEADS, ML_HEAD_DIM), f32),
                  jnp.zeros((B, ML_HEADS), f32))

        def to_chunks(t):
            return jnp.moveaxis(t.reshape(t.shape[:2] + (nc, CHUNK) + t.shape[3:]), 2, 0)

        (C_new, n_new, m_new), hs = lax.scan(
            mlstm_chunk, state0,
            (to_chunks(qm), to_chunks(km), to_chunks(vm), to_chunks(ig), to_chunks(lf)))
        h = jnp.moveaxis(hs, 0, 2).reshape(B, ML_HEADS, L, ML_HEAD_DIM)
    else:
        (C_new, n_new, m_new), h = mlstm_chunk(
            (past[2].astype(f32), past[3].astype(f32), past[4].astype(f32)), (qm, km, vm, ig, lf))
    h = jnp.transpose(h, (0, 2, 1, 3))
    hn = rmsnorm(h, ml_norm_g.reshape(ML_HEADS, ML_HEAD_DIM)).reshape(B, L, ML_WIDTH).astype(x.dtype)
    o_ml = (hn + ml_skip * cc) * jax.nn.sigmoid(mo)

    u = jax.nn.gelu(cu, approximate=False)
    vcm = rmsnorm(jax.nn.gelu(cv, approximate=False), cm_norm_g)
    ws = cm_ws * jnp.tril(jnp.ones((CM_CHUNK, CM_CHUNK), cm_ws.dtype))
    vg = vcm.reshape(B, L, CM_GROUPS, CM_GROUP_DIM)
    if past is None:
        nchunk = L // CM_CHUNK
        vgc = vg.reshape(B, nchunk, CM_CHUNK, CM_GROUPS, CM_GROUP_DIM)
        mixed = jnp.einsum('gts,bnsgd->bntgd', ws, vgc) + cm_b.T[None, None, :, :, None]
        mixed = mixed.reshape(B, L, CM_WIDTH)
    else:
        mixed = jnp.einsum('gts,bsgd->btgd', ws[:, :L, :L], vg) + cm_b[:, :L].T[None, :, :, None]
        mixed = mixed.reshape(B, L, CM_WIDTH)
    o_cm = u * mixed

    x = x + jnp.concatenate([o_da, o_ml, o_cm], axis=-1) @ w_out
    xn2 = rmsnorm(x, norm2_g).reshape(B * L, D_MODEL)
    x = x + peer(xn2, peer_wq, peer_keys, peer_u, peer_v).reshape(B, L, D_MODEL)
    if past is None:
        return x, (k_rows, v_rows, C_new, n_new, m_new, conv_new)
    return x, (k_rows, v_rows, C_new, n_new, m_new, conv_new, vcm)


def setup_inputs(seed: int = 0) -> dict:
    key = jax.random.key(seed)
    ks = jax.random.split(key, 32)
    nrm = jax.random.normal
    f32 = jnp.float32
    fb = jnp.linspace(3.0, 6.0, ML_HEADS, dtype=f32)
    gate_b = jnp.stack([0.1 * nrm(ks[12], (DEPTH, ML_HEADS), f32),
                        fb[None, :] + 0.1 * nrm(ks[13], (DEPTH, ML_HEADS), f32)], axis=1)
    return {
        "x_prompt": nrm(ks[0], (BATCH, SEQ, D_MODEL), f32),
        "x_sample": nrm(ks[1], (DEC_BATCH, DEC_SEQ, D_MODEL), f32),
        "cache_k": nrm(ks[2], (DEPTH, DEC_BATCH, PAST_LEN, DA_HEADS, 2 * DA_HEAD_DIM), f32),
        "cache_v": nrm(ks[3], (DEPTH, DEC_BATCH, PAST_LEN, DA_HEADS, DA_VDIM), f32),
        "state_mlstm_c": 0.3 * nrm(ks[4], (DEPTH, DEC_BATCH, ML_HEADS, ML_HEAD_DIM, ML_HEAD_DIM), f32),
        "state_mlstm_n": 0.3 * nrm(ks[5], (DEPTH, DEC_BATCH, ML_HEADS, ML_HEAD_DIM), f32),
        "state_mlstm_m": 0.5 * nrm(ks[6], (DEPTH, DEC_BATCH, ML_HEADS), f32),
        "state_mlstm_conv": nrm(ks[7], (DEPTH, DEC_BATCH, ML_CONV - 1, ML_WIDTH), f32),
        "norm1_g": 1.0 + 0.1 * nrm(ks[8], (DEPTH, D_MODEL), f32),
        "w_in": nrm(ks[9], (DEPTH, D_MODEL, IN_WIDTH), f32) * D_MODEL ** -0.5,
        "da_lambda": 0.1 * nrm(ks[10], (DEPTH, 4, DA_HEAD_DIM), f32),
        "da_subln_g": 1.0 + 0.1 * nrm(ks[11], (DEPTH, DA_VDIM), f32),
        "rel_bias_table": 0.5 * nrm(ks[14], (REL_BUCKETS, DA_HEADS), f32),
        "ml_conv_w": nrm(ks[15], (DEPTH, ML_CONV, ML_WIDTH), f32) * ML_CONV ** -0.5,
        "ml_conv_b": 0.05 * nrm(ks[16], (DEPTH, ML_WIDTH), f32),
        "ml_wq": nrm(ks[17], (DEPTH, ML_HEADS, ML_HEAD_DIM, ML_HEAD_DIM), f32) * ML_HEAD_DIM ** -0.5,
        "ml_wk": nrm(ks[18], (DEPTH, ML_HEADS, ML_HEAD_DIM, ML_HEAD_DIM), f32) * ML_HEAD_DIM ** -0.5,
        "ml_gate_b": gate_b,
        "ml_norm_g": 1.0 + 0.1 * nrm(ks[19], (DEPTH, ML_WIDTH), f32),
        "ml_skip": 1.0 + 0.1 * nrm(ks[20], (DEPTH, ML_WIDTH), f32),
        "cm_norm_g": 1.0 + 0.1 * nrm(ks[21], (DEPTH, CM_WIDTH), f32),
        "cm_ws": nrm(ks[22], (DEPTH, CM_GROUPS, CM_CHUNK, CM_CHUNK), f32) * CM_CHUNK ** -0.5,
        "cm_b": 1.0 + 0.1 * nrm(ks[23], (DEPTH, CM_GROUPS, CM_CHUNK), f32),
        "w_out": nrm(ks[24], (DEPTH, MIX_WIDTH, D_MODEL), f32) * MIX_WIDTH ** -0.5,
        "norm2_g": 1.0 + 0.1 * nrm(ks[25], (DEPTH, D_MODEL), f32),
        "peer_wq": nrm(ks[26], (DEPTH, D_MODEL, PEER_HEADS * PEER_QDIM), f32) * D_MODEL ** -0.5,
        "peer_keys": nrm(ks[27], (DEPTH, PEER_HEADS, 2, PEER_KEYS, PEER_HALF), f32) * PEER_HALF ** -0.5,
        "peer_u": nrm(ks[28], (DEPTH, PEER_EXPERTS, D_MODEL), f32) * D_MODEL ** -0.5,
        "peer_v": 0.1 * nrm(ks[29], (DEPTH, PEER_EXPERTS, D_MODEL), f32),
        "final_g": 1.0 + 0.1 * nrm(ks[30], (D_MODEL,), f32),
    }


def reference(x_prompt, x_sample, cache_k, cache_v, state_mlstm_c, state_mlstm_n, state_mlstm_m,
              state_mlstm_conv, norm1_g, w_in, da_lambda, da_subln_g, rel_bias_table, ml_conv_w,
              ml_conv_b, ml_wq, ml_wk, ml_gate_b, ml_norm_g, ml_skip, cm_norm_g, cm_ws, cm_b,
              w_out, norm2_g, peer_wq, peer_keys, peer_u, peer_v, final_g):
    hp, hs = x_prompt, x_sample
    st_p, st_s = [], []
    for l in range(DEPTH):
        lw = (norm1_g[l], w_in[l], da_lambda[l], da_subln_g[l], ml_conv_w[l], ml_conv_b[l],
              ml_wq[l], ml_wk[l], ml_gate_b[l], ml_norm_g[l], ml_skip[l], cm_norm_g[l], cm_ws[l],
              cm_b[l], w_out[l], norm2_g[l], peer_wq[l], peer_keys[l], peer_u[l], peer_v[l])
        hp, sp = layer(hp, None, l, rel_bias_table, *lw)
        past = (cache_k[l], cache_v[l], state_mlstm_c[l], state_mlstm_n[l], state_mlstm_m[l],
                state_mlstm_conv[l])
        hs, ss = layer(hs, past, l, rel_bias_table, *lw)
        st_p.append(sp)
        st_s.append(ss)
    y_prompt = rmsnorm(hp, final_g)
    y_sample = rmsnorm(hs, final_g)
    new_k_prompt = jnp.stack([s[0] for s in st_p])
    new_v_prompt = jnp.stack([s[1] for s in st_p])
    new_c_prompt = jnp.stack([s[2] for s in st_p])
    new_n_prompt = jnp.stack([s[3] for s in st_p])
    new_m_prompt = jnp.stack([s[4] for s in st_p])
    new_conv_prompt = jnp.stack([s[5] for s in st_p])
    new_k_sample = jnp.stack([s[0] for s in st_s])
    new_v_sample = jnp.stack([s[1] for s in st_s])
    new_c_sample = jnp.stack([s[2] for s in st_s])
    new_n_sample = jnp.stack([s[3] for s in st_s])
    new_m_sample = jnp.stack([s[4] for s in st_s])
    new_conv_sample = jnp.stack([s[5] for s in st_s])
    new_cmv_sample = jnp.stack([s[6] for s in st_s])
    return (y_prompt, y_sample, new_k_prompt, new_v_prompt, new_c_prompt, new_n_prompt, new_m_prompt,
            new_conv_prompt, new_k_sample, new_v_sample, new_c_sample, new_n_sample, new_m_sample,
            new_conv_sample, new_cmv_sample)
```

```python
import functools
import math

import numpy as np
import jax
import jax.numpy as jnp
from jax import lax
from jax.experimental import pallas as pl
from jax.experimental.pallas import tpu as pltpu

F32 = jnp.float32
BF16 = jnp.bfloat16

D_MODEL = 1024
CHUNK = 64
DA_HEADS = 4
DA_HEAD_DIM = 64
DA_VDIM = 128
DA_WIDTH = 512
ML_HEADS = 4
ML_HEAD_DIM = 64
ML_WIDTH = 256
ML_CONV = 4
CM_GROUPS = 4
CM_WIDTH = 256
CM_GROUP_DIM = 64
CM_CHUNK = 128
PEER_HEADS = 8
PEER_KEYS = 128
PEER_EXPERTS = PEER_KEYS * PEER_KEYS
PEER_QDIM = 256
PEER_HALF = 128
PEER_TOPK = 16
REL_BUCKETS = 32
REL_MAX_DIST = 128
EPS = 1e-6
NEG_INF = -1e30

LANES = 128
ML_PAD = ML_HEADS * LANES
VMEM_LIMIT = 48 * 1024 * 1024

_C_Q, _C_K, _C_V = 0, 512, 1024
_C_MC, _C_MV, _C_MO = 1536, 2048, 2560
_C_CU, _C_CV, _C_G = 3072, 3328, 3584
IN_PAD = 3712


def _cparams(sem):
    return pltpu.CompilerParams(dimension_semantics=sem, vmem_limit_bytes=VMEM_LIMIT)


def _gelu(x):
    return 0.5 * x * (1.0 + lax.erf(x * (2.0 ** -0.5)))


def _dot_nt(a, b):
    return lax.dot_general(a, b, (((1,), (1,)), ((), ())), preferred_element_type=F32)


def _dot_tn(a, b):
    return lax.dot_general(a, b, (((0,), (0,)), ((), ())), preferred_element_type=F32)


def _inproj_kernel(x_ref, g_ref, w_ref, q_ref, kf_ref, vf_ref, kb_ref, vb_ref,
                   mc_ref, mv_ref, mo_ref, cu_ref, cv_ref, gc_ref):
    x = x_ref[...]
    xn = x * lax.rsqrt(jnp.mean(x * x, axis=-1, keepdims=True) + EPS) * g_ref[...]
    xb = xn.astype(BF16)

    def proj(lo, hi):
        return jnp.dot(xb, w_ref[:, lo:hi], preferred_element_type=F32)

    q_ref[...] = (proj(_C_Q, _C_K) * (DA_HEAD_DIM ** -0.5)).astype(BF16)
    k = proj(_C_K, _C_V)
    kf_ref[...] = k
    kb_ref[...] = k.astype(BF16)
    v = proj(_C_V, _C_MC)
    vf_ref[...] = v
    vb_ref[...] = v.astype(BF16)
    mc_ref[...] = proj(_C_MC, _C_MV)
    mv_ref[...] = proj(_C_MV, _C_MO)
    mo_ref[...] = proj(_C_MO, _C_CU)
    cu_ref[...] = proj(_C_CU, _C_CV)
    cv_ref[...] = proj(_C_CV, _C_G)
    gc_ref[...] = proj(_C_G, IN_PAD)


def _in_proj(x2d, g, w_pad, tm):
    T = x2d.shape[0]
    row = lambda n: pl.BlockSpec((tm, n), lambda i: (i, 0))
    full = lambda a: pl.BlockSpec(a.shape, lambda i: (0,) * a.ndim)
    outs = [(DA_WIDTH, BF16), (DA_WIDTH, F32), (DA_WIDTH, F32), (DA_WIDTH, BF16), (DA_WIDTH, BF16),
            (ML_PAD, F32), (ML_PAD, F32), (ML_PAD, F32), (CM_WIDTH, F32), (CM_WIDTH, F32), (LANES, F32)]
    return pl.pallas_call(
        _inproj_kernel,
        grid=(T // tm,),
        in_specs=[row(D_MODEL), full(g), full(w_pad)],
        out_specs=[row(n) for n, _ in outs],
        out_shape=[jax.ShapeDtypeStruct((T, n), dt) for n, dt in outs],
        compiler_params=_cparams(("parallel",)),
        name="in_proj",
    )(x2d, g, w_pad)


def _rel_bucket_np(rel):
    half = REL_BUCKETS // 2
    max_exact = half // 2
    ret = np.where(rel > 0, half, 0)
    n = np.abs(rel)
    nf = np.maximum(n, 1).astype(np.float32)
    large = max_exact + (np.log(nf / np.float32(max_exact)) / np.float32(math.log(REL_MAX_DIST / max_exact))
                         * np.float32(half - max_exact)).astype(np.int32)
    large = np.minimum(large, half - 1)
    return (ret + np.where(n < max_exact, n, large)).astype(np.int32)


def _bias_kernel(tab_ref, idx_ref, o_ref, *, shift_bucket):
    for n in range(idx_ref.shape[0]):
        idx = idx_ref[n]
        for h in range(DA_HEADS):
            acc = jnp.zeros(idx.shape, F32)
            for b in range(REL_BUCKETS):
                acc = jnp.where(idx == b, tab_ref[b, h], acc)
            if shift_bucket is not None:
                acc = acc - tab_ref[shift_bucket, h]
            o_ref[n, h] = acc


def _bias_tiles(table, idx_np, shift_bucket):
    n, r, c = idx_np.shape
    return pl.pallas_call(
        functools.partial(_bias_kernel, shift_bucket=shift_bucket),
        in_specs=[pl.BlockSpec(memory_space=pltpu.SMEM), pl.BlockSpec(memory_space=pltpu.VMEM)],
        out_specs=pl.BlockSpec(memory_space=pltpu.VMEM),
        out_shape=jax.ShapeDtypeStruct((n, DA_HEADS, r, c), F32),
        name="rel_bias",
    )(table, jnp.asarray(idx_np))


def _lambda(lam_ref, lam_init):
    lp = lam_ref[...]
    a = jnp.sum(lp[0:1] * lp[1:2], axis=-1, keepdims=True)
    b = jnp.sum(lp[2:3] * lp[3:4], axis=-1, keepdims=True)
    return jnp.exp(a) - jnp.exp(b) + lam_init


def _subln(o1, l1, o2, l2, lam, g, lam_init):
    o = o1 / l1 - lam * (o2 / l2)
    o = o * lax.rsqrt(jnp.mean(o * o, axis=-1, keepdims=True) + EPS) * g
    return o * (1.0 - lam_init)


def _attn_prompt_kernel(lam_ref, q_ref, k_ref, v_ref, bias_ref, g_ref, o_ref, *, tq, lam_init):
    qb = pl.program_id(1)
    lam = _lambda(lam_ref, lam_init)
    first = lax.broadcasted_iota(jnp.int32, (1, LANES), 1) < DA_HEAD_DIM
    rows = lax.broadcasted_iota(jnp.int32, (tq, tq), 0)
    cols = lax.broadcasted_iota(jnp.int32, (tq, tq), 1)
    diag_mask = (cols // CHUNK) <= (rows // CHUNK)
    has_prev = qb >= 1
    prev_off = pl.multiple_of(jnp.maximum(qb - 1, 0) * tq, tq)
    diag_off = pl.multiple_of(qb * tq, tq)
    g = g_ref[...]

    for h in range(DA_HEADS):
        sl = slice(LANES * h, LANES * (h + 1))
        q = q_ref[0, :, sl]
        res = []
        for c in range(2):
            qc = jnp.where(first if c == 0 else jnp.logical_not(first), q, jnp.zeros_like(q))

            def blk(koff, carry, bias=None, mask=None, qc=qc, sl=sl):
                m, l, acc = carry
                k = k_ref[0, pl.ds(koff, tq), sl]
                v = v_ref[0, pl.ds(koff, tq), sl]
                s = _dot_nt(qc, k)
                if bias is not None:
                    s = s + bias
                if mask is not None:
                    s = jnp.where(mask, s, NEG_INF)
                m_new = jnp.maximum(m, jnp.max(s, axis=-1, keepdims=True))
                alpha = jnp.exp(m - m_new)
                p = jnp.exp(s - m_new)
                l = alpha * l + jnp.sum(p, axis=-1, keepdims=True)
                acc = alpha * acc + jnp.dot(p.astype(BF16), v, preferred_element_type=F32)
                return m_new, l, acc

            carry = (jnp.full((tq, 1), NEG_INF, F32), jnp.zeros((tq, 1), F32), jnp.zeros((tq, LANES), F32))
            carry = blk(diag_off, carry, bias_ref[0, h], diag_mask)
            carry = blk(prev_off, carry, bias_ref[1, h], has_prev)
            carry = lax.fori_loop(
                0, jnp.maximum(qb - 1, 0),
                lambda j, cr: blk(pl.multiple_of(j * tq, tq), cr), carry)
            res.append(carry)
        (_, l1, a1), (_, l2, a2) = res
        o_ref[0, :, sl] = _subln(a1, l1, a2, l2, lam, g, lam_init).astype(o_ref.dtype)


def _attn_prompt(q, k, v, bias, lam_p, g, lam_init, tq):
    B, S, _ = q.shape
    return pl.pallas_call(
        functools.partial(_attn_prompt_kernel, tq=tq, lam_init=lam_init),
        grid=(B, S // tq),
        in_specs=[
            pl.BlockSpec(lam_p.shape, lambda b, i: (0, 0)),
            pl.BlockSpec((1, tq, DA_WIDTH), lambda b, i: (b, i, 0)),
            pl.BlockSpec((1, S, DA_WIDTH), lambda b, i: (b, 0, 0)),
            pl.BlockSpec((1, S, DA_WIDTH), lambda b, i: (b, 0, 0)),
            pl.BlockSpec(bias.shape, lambda b, i: (0, 0, 0, 0)),
            pl.BlockSpec(g.shape, lambda b, i: (0, 0)),
        ],
        out_specs=pl.BlockSpec((1, tq, DA_WIDTH), lambda b, i: (b, i, 0)),
        out_shape=jax.ShapeDtypeStruct((B, S, DA_WIDTH), BF16),
        compiler_params=_cparams(("parallel", "arbitrary")),
        name="attn_prompt",
    )(lam_p, q, k, v, bias, g)


def _attn_step_kernel(lam_ref, q_ref, kn_ref, vn_ref, kp_ref, vp_ref, bp_ref, bn_ref, g_ref, o_ref, *, lam_init):
    lam = _lambda(lam_ref, lam_init)
    first = lax.broadcasted_iota(jnp.int32, (1, LANES), 1) < DA_HEAD_DIM
    g = g_ref[...]
    for h in range(DA_HEADS):
        sl = slice(LANES * h, LANES * (h + 1))
        q = q_ref[0, :, sl]
        kn = kn_ref[0, :, sl]
        vn = vn_ref[0, :, sl]
        kp = kp_ref[0, :, sl].astype(BF16)
        vp = vp_ref[0, :, sl].astype(BF16)
        res = []
        for c in range(2):
            qc = jnp.where(first if c == 0 else jnp.logical_not(first), q, jnp.zeros_like(q))
            sp = _dot_nt(qc, kp) + bp_ref[0, h]
            sn = _dot_nt(qc, kn) + bn_ref[0, h]
            m = jnp.maximum(jnp.max(sp, axis=-1, keepdims=True), jnp.max(sn, axis=-1, keepdims=True))
            pp = jnp.exp(sp - m)
            pn = jnp.exp(sn - m)
            l = jnp.sum(pp, axis=-1, keepdims=True) + jnp.sum(pn, axis=-1, keepdims=True)
            acc = (jnp.dot(pp.astype(BF16), vp, preferred_element_type=F32)
                   + jnp.dot(pn.astype(BF16), vn, preferred_element_type=F32))
            res.append((l, acc))
        (l1, a1), (l2, a2) = res
        o_ref[0, :, sl] = _subln(a1, l1, a2, l2, lam, g, lam_init).astype(o_ref.dtype)


def _attn_step(q, kn, vn, kp, vp, bias_p, bias_n, lam_p, g, lam_init):
    B, L, _ = q.shape
    P = kp.shape[1]
    c3 = lambda b: (b, 0, 0)
    z4 = lambda b: (0, 0, 0, 0)
    return pl.pallas_call(
        functools.partial(_attn_step_kernel, lam_init=lam_init),
        grid=(B,),
        in_specs=[
            pl.BlockSpec(lam_p.shape, lambda b: (0, 0)),
            pl.BlockSpec((1, L, DA_WIDTH), c3), pl.BlockSpec((1, L, DA_WIDTH), c3), pl.BlockSpec((1, L, DA_WIDTH), c3),
            pl.BlockSpec((1, P, DA_WIDTH), c3), pl.BlockSpec((1, P, DA_WIDTH), c3),
            pl.BlockSpec(bias_p.shape, z4), pl.BlockSpec(bias_n.shape, z4),
            pl.BlockSpec(g.shape, lambda b: (0, 0)),
        ],
        out_specs=pl.BlockSpec((1, L, DA_WIDTH), c3),
        out_shape=jax.ShapeDtypeStruct((B, L, DA_WIDTH), BF16),
        compiler_params=_cparams(("parallel",)),
        name="attn_step",
    )(lam_p, q, kn, vn, kp, vp, bias_p, bias_n, g)


def _log_sigmoid(x):
    return jnp.minimum(x, 0.0) - jnp.log(1.0 + jnp.exp(-jnp.abs(x)))


def _mlstm_kernel(gb_ref, mc_ref, mv_ref, mo_ref, gc_ref, cw_ref, cb_ref, wq_ref, wk_ref, ng_ref, sk_ref,
                  c0_ref, n0_ref, m0_ref, cv0_ref,
                  o_ref, co_ref, no_ref, mo2_ref, cvo_ref,
                  xbuf, c_sc, n_sc, m_sc):
    j = pl.program_id(1)
    L = CHUNK

    @pl.when(j == 0)
    def _():
        xbuf[5:8, :] = cv0_ref[0]
        c_sc[...] = c0_ref[0]
        n_sc[...] = n0_ref[0]
        m_sc[...] = m0_ref[0]

    x = mc_ref[0]
    xbuf[8:8 + L, :] = x
    y = (cb_ref[...] + cw_ref[3:4, :] * x + cw_ref[2:3, :] * xbuf[7:7 + L, :]
         + cw_ref[1:2, :] * xbuf[6:6 + L, :] + cw_ref[0:1, :] * xbuf[5:5 + L, :])
    cc = y * jax.nn.sigmoid(y)
    tail = xbuf[5 + L:8 + L, :]
    xbuf[5:8, :] = tail
    cvo_ref[0] = tail

    gcol = gc_ref[0]
    gt = jnp.concatenate([gcol, jnp.zeros((LANES - L, LANES), F32)], axis=0).T
    r = lax.broadcasted_iota(jnp.int32, (L, L), 0)
    s = lax.broadcasted_iota(jnp.int32, (L, L), 1)
    tril = s <= r

    for h in range(ML_HEADS):
        sl = slice(LANES * h, LANES * (h + 1))
        b_i = gb_ref[0, h]
        b_f = gb_ref[1, h]
        ig_c = gcol[:, h:h + 1] + b_i
        lf_c = _log_sigmoid(gcol[:, 4 + h:5 + h] + b_f)
        ig_r = gt[h:h + 1, 0:L] + b_i
        lf_r = _log_sigmoid(gt[4 + h:5 + h, 0:L] + b_f)
        f_c = jnp.sum(jnp.where(tril, lf_r, 0.0), axis=1, keepdims=True)
        f_r = jnp.sum(jnp.where(r <= s, lf_c, 0.0), axis=0, keepdims=True)
        m_prev = m_sc[h][0:1, 0:1]
        dmat = jnp.where(tril, f_c - f_r + ig_r, NEG_INF)
        inter = f_c + m_prev
        m_t = jnp.maximum(inter, jnp.max(dmat, axis=1, keepdims=True))
        cch = cc[:, sl]
        qh = jnp.dot(cch, wq_ref[h], preferred_element_type=F32)
        kh = jnp.dot(cch, wk_ref[h], preferred_element_type=F32) * (ML_HEAD_DIM ** -0.5)
        vh = mv_ref[0, :, sl]
        sw = _dot_nt(qh, kh) * jnp.exp(dmat - m_t)
        iw = jnp.exp(inter - m_t)
        c_prev = c_sc[h]
        n_prev = n_sc[h]
        num = jnp.dot(sw, vh, preferred_element_type=F32) + iw * jnp.dot(qh, c_prev, preferred_element_type=F32)
        den = jnp.sum(sw, axis=1, keepdims=True) + iw * jnp.sum(qh * n_prev, axis=1, keepdims=True)
        hout = num / jnp.maximum(jnp.abs(den), jnp.exp(-m_t))
        fl = f_c[L - 1:L, :]
        tail_r = fl - f_r + ig_r
        tail_c = fl - f_c + ig_c
        m_new = jnp.maximum(fl + m_prev, jnp.max(tail_r, axis=1, keepdims=True))
        wc = jnp.exp(fl + m_prev - m_new)
        kw = kh * jnp.exp(tail_c - m_new)
        c_new = wc * c_prev + _dot_tn(kw, vh)
        n_new = wc * n_prev + jnp.sum(kw, axis=0, keepdims=True)
        c_sc[h] = c_new
        n_sc[h] = n_new
        m_sc[h] = jnp.broadcast_to(m_new, (1, LANES))
        co_ref[0, h] = c_new
        no_ref[0, h] = n_new
        mo2_ref[0, h] = jnp.broadcast_to(m_new, (1, LANES))
        hn = hout * lax.rsqrt(jnp.sum(hout * hout, axis=1, keepdims=True) * (1.0 / ML_HEAD_DIM) + EPS) * ng_ref[:, sl]
        o_ref[0, :, sl] = ((hn + sk_ref[:, sl] * cch) * jax.nn.sigmoid(mo_ref[0, :, sl])).astype(o_ref.dtype)


def _mlstm(gate_b, mc, mv, mo, gc, cw, cb, wq, wk, ng, sk, c0, n0, m0, cv0):
    B, L, _ = mc.shape
    nc = L // CHUNK
    tok = lambda n: pl.BlockSpec((1, CHUNK, n), lambda b, j: (b, j, 0))
    full = lambda a: pl.BlockSpec(a.shape, lambda b, j: (0,) * a.ndim)
    st4 = lambda a: pl.BlockSpec((1,) + a.shape[1:], lambda b, j: (b,) + (0,) * (a.ndim - 1))
    return pl.pallas_call(
        _mlstm_kernel,
        grid=(B, nc),
        in_specs=[pl.BlockSpec(memory_space=pltpu.SMEM),
                  tok(ML_PAD), tok(ML_PAD), tok(ML_PAD), tok(LANES),
                  full(cw), full(cb), full(wq), full(wk), full(ng), full(sk),
                  st4(c0), st4(n0), st4(m0), st4(cv0)],
        out_specs=[tok(ML_PAD), st4(c0), st4(n0), st4(m0), st4(cv0)],
        out_shape=[jax.ShapeDtypeStruct((B, L, ML_PAD), BF16),
                   jax.ShapeDtypeStruct(c0.shape, F32), jax.ShapeDtypeStruct(n0.shape, F32),
                   jax.ShapeDtypeStruct(m0.shape, F32), jax.ShapeDtypeStruct(cv0.shape, F32)],
        scratch_shapes=[pltpu.VMEM((8 + CHUNK, ML_PAD), F32),
                        pltpu.VMEM((ML_HEADS, LANES, LANES), F32),
                        pltpu.VMEM((ML_HEADS, 1, LANES), F32),
                        pltpu.VMEM((ML_HEADS, 1, LANES), F32)],
        compiler_params=_cparams(("parallel", "arbitrary")),
        name="mlstm",
    )(gate_b, mc, mv, mo, gc, cw, cb, wq, wk, ng, sk, c0, n0, m0, cv0)


def _mixout_kernel(x_ref, oda_ref, oml_ref, cu_ref, cv_ref, cmg_ref, ws_ref, bm_ref,
                   wda_ref, wml_ref, wcm_ref, x1_ref, vcm_ref, *, lc):
    tm = x_ref.shape[0]
    u = _gelu(cu_ref[...])
    gv = _gelu(cv_ref[...])
    vcm = gv * lax.rsqrt(jnp.mean(gv * gv, axis=-1, keepdims=True) + EPS) * cmg_ref[...]
    vcm_ref[...] = vcm
    r = lax.broadcasted_iota(jnp.int32, (lc, lc), 0)
    c = lax.broadcasted_iota(jnp.int32, (lc, lc), 1)
    grp = lax.broadcasted_iota(jnp.int32, (1, CM_WIDTH), 1) // CM_GROUP_DIM
    wsm = [jnp.where(c <= r, ws_ref[g, 0:lc, 0:lc], 0.0).astype(BF16) for g in range(CM_GROUPS)]
    pieces = []
    for ci in range(tm // lc):
        vch = vcm[ci * lc:(ci + 1) * lc].astype(BF16)
        mixed = bm_ref[0:lc, :]
        for g in range(CM_GROUPS):
            mixed = mixed + jnp.where(grp == g, jnp.dot(wsm[g], vch, preferred_element_type=F32), 0.0)
        pieces.append(u[ci * lc:(ci + 1) * lc] * mixed)
    ocm = jnp.concatenate(pieces, axis=0) if len(pieces) > 1 else pieces[0]
    y = (jnp.dot(oda_ref[...], wda_ref[...], preferred_element_type=F32)
         + jnp.dot(oml_ref[...], wml_ref[...], preferred_element_type=F32)
         + jnp.dot(ocm.astype(BF16), wcm_ref[...], preferred_element_type=F32))
    x1_ref[...] = x_ref[...] + y


def _mix_out(x2d, oda, oml, cu, cv, cmg, ws, bmat, wda, wml, wcm, tm, lc):
    T = x2d.shape[0]
    row = lambda n: pl.BlockSpec((tm, n), lambda i: (i, 0))
    full = lambda a: pl.BlockSpec(a.shape, lambda i: (0,) * a.ndim)
    return pl.pallas_call(
        functools.partial(_mixout_kernel, lc=lc),
        grid=(T // tm,),
        in_specs=[row(D_MODEL), row(DA_WIDTH), row(ML_PAD), row(CM_WIDTH), row(CM_WIDTH),
                  full(cmg), full(ws), full(bmat), full(wda), full(wml), full(wcm)],
        out_specs=[row(D_MODEL), row(CM_WIDTH)],
        out_shape=[jax.ShapeDtypeStruct((T, D_MODEL), F32), jax.ShapeDtypeStruct((T, CM_WIDTH), F32)],
        compiler_params=_cparams(("parallel",)),
        name="mix_out",
    )(x2d, oda, oml, cu, cv, cmg, ws, bmat, wda, wml, wcm)


def _peer_query_kernel(x_ref, g_ref, wq_ref, keys_ref, xn_ref, st_ref):
    x = x_ref[...]
    xn = (x * lax.rsqrt(jnp.mean(x * x, axis=-1, keepdims=True) + EPS) * g_ref[...]).astype(BF16)
    xn_ref[...] = xn
    for h in range(PEER_HEADS):
        q = jnp.dot(xn, wq_ref[:, PEER_QDIM * h:PEER_QDIM * (h + 1)], preferred_element_type=F32)
        q = q * lax.rsqrt(jnp.mean(q * q, axis=-1, keepdims=True) + EPS)
        for c in range(2):
            qc = q[:, PEER_HALF * c:PEER_HALF * (c + 1)].astype(BF16)
            st_ref[h, c] = _dot_nt(keys_ref[h, c], qc)


def _peer_query(x1, g, wq, keys, tm):
    T = x1.shape[0]
    full = lambda a: pl.BlockSpec(a.shape, lambda i: (0,) * a.ndim)
    return pl.pallas_call(
        _peer_query_kernel,
        grid=(T // tm,),
        in_specs=[pl.BlockSpec((tm, D_MODEL), lambda i: (i, 0)), full(g), full(wq), full(keys)],
        out_specs=[pl.BlockSpec((tm, D_MODEL), lambda i: (i, 0)),
                   pl.BlockSpec((PEER_HEADS, 2, PEER_KEYS, tm), lambda i: (0, 0, 0, i))],
        out_shape=[jax.ShapeDtypeStruct((T, D_MODEL), BF16),
                   jax.ShapeDtypeStruct((PEER_HEADS, 2, PEER_KEYS, T), F32)],
        compiler_params=_cparams(("parallel",)),
        name="peer_query",
    )(x1, g, wq, keys)


_PAIRS = [(a, b) for a in range(PEER_TOPK) for b in range(PEER_TOPK) if (a + 1) * (b + 1) <= PEER_TOPK]
_NCAND = ((len(_PAIRS) + 7) // 8) * 8


def _peer_select_kernel(st_ref, d_ref, a_ref, b_ref, vals, cand):
    tt = st_ref.shape[-1]
    for h in range(PEER_HEADS):
        for c in range(2):
            cur = st_ref[h, c]
            for i in range(PEER_TOPK):
                m = jnp.max(cur, axis=0, keepdims=True)
                vals[c, i:i + 1, :] = m
                cur = jnp.where(cur == m, NEG_INF, cur)
        cand[...] = jnp.full((_NCAND, tt), NEG_INF, F32)
        for i, (a, b) in enumerate(_PAIRS):
            cand[i:i + 1, :] = vals[0, a:a + 1, :] + vals[1, b:b + 1, :]
        top1 = vals[0, 0:1, :]
        top2 = vals[1, 0:1, :]
        best = top1 + top2
        cur = cand[...]
        z = jnp.zeros((1, tt), F32)
        thr = best
        for i in range(PEER_TOPK):
            thr = jnp.max(cur, axis=0, keepdims=True)
            z = z + jnp.exp(thr - best)
            cur = jnp.where(cur == thr, NEG_INF, cur)
        s1 = st_ref[h, 0]
        d_ref[h] = thr - s1
        a_ref[h] = jnp.exp(s1 - top1) / z
        b_ref[h] = jnp.exp(st_ref[h, 1] - top2)


def _peer_select(st, tt):
    T = st.shape[-1]
    spec = pl.BlockSpec((PEER_HEADS, PEER_KEYS, tt), lambda i: (0, 0, i))
    shp = jax.ShapeDtypeStruct((PEER_HEADS, PEER_KEYS, T), F32)
    return pl.pallas_call(
        _peer_select_kernel,
        grid=(T // tt,),
        in_specs=[pl.BlockSpec((PEER_HEADS, 2, PEER_KEYS, tt), lambda i: (0, 0, 0, i))],
        out_specs=[spec, spec, spec],
        out_shape=[shp, shp, shp],
        scratch_shapes=[pltpu.VMEM((2, PEER_TOPK, tt), F32), pltpu.VMEM((_NCAND, tt), F32)],
        compiler_params=_cparams(("parallel",)),
        name="peer_select",
    )(st)


def _peer_dense_kernel(x1_ref, xn_ref, s2_ref, d_ref, a_ref, b_ref, u_ref, vt_ref, fg_ref, o_ref,
                       acc_ref, at_ref, *, rows, final_norm):
    j = pl.program_id(1)

    @pl.when(j == 0)
    def _():
        acc_ref[...] = jnp.zeros_like(acc_ref)

    gt = _gelu(_dot_nt(u_ref[...], xn_ref[...]))
    for r in range(rows):
        i1 = j * rows + r
        w = jnp.zeros((PEER_KEYS, gt.shape[1]), F32)
        for h in range(PEER_HEADS):
            d = d_ref[h, pl.ds(i1, 1), :]
            a = a_ref[h, pl.ds(i1, 1), :]
            w = w + jnp.where(s2_ref[h] >= d, b_ref[h], 0.0) * a
        at_ref[r * PEER_KEYS:(r + 1) * PEER_KEYS, :] = (w * gt[r * PEER_KEYS:(r + 1) * PEER_KEYS]).astype(BF16)
    acc_ref[...] += jnp.dot(vt_ref[...], at_ref[...], preferred_element_type=F32)

    @pl.when(j == pl.num_programs(1) - 1)
    def _():
        x2 = x1_ref[...] + acc_ref[...].T
        if final_norm:
            x2 = x2 * lax.rsqrt(jnp.mean(x2 * x2, axis=-1, keepdims=True) + EPS) * fg_ref[...]
        o_ref[...] = x2


def _peer_dense(x1, xn, st, d, a, b, u_bf, vt_bf, fg, tt, eb, final_norm):
    T = x1.shape[0]
    rows = eb // PEER_KEYS
    hk = pl.BlockSpec((PEER_HEADS, PEER_KEYS, tt), lambda i, j: (0, 0, i))
    return pl.pallas_call(
        functools.partial(_peer_dense_kernel, rows=rows, final_norm=final_norm),
        grid=(T // tt, PEER_EXPERTS // eb),
        in_specs=[pl.BlockSpec((tt, D_MODEL), lambda i, j: (i, 0)),
                  pl.BlockSpec((tt, D_MODEL), lambda i, j: (i, 0)),
                  pl.BlockSpec((PEER_HEADS, None, PEER_KEYS, tt), lambda i, j: (0, 1, 0, i)),
                  hk, hk, hk,
                  pl.BlockSpec((eb, D_MODEL), lambda i, j: (j, 0)),
                  pl.BlockSpec((D_MODEL, eb), lambda i, j: (0, j)),
                  pl.BlockSpec(fg.shape, lambda i, j: (0, 0))],
        out_specs=pl.BlockSpec((tt, D_MODEL), lambda i, j: (i, 0)),
        out_shape=jax.ShapeDtypeStruct((T, D_MODEL), F32),
        scratch_shapes=[pltpu.VMEM((D_MODEL, tt), F32), pltpu.VMEM((eb, tt), BF16)],
        compiler_params=_cparams(("parallel", "arbitrary")),
        name="peer_dense",
    )(x1, xn, st, d, a, b, u_bf, vt_bf, fg)


def _pad_heads(w, axis=-1):
    w = jnp.moveaxis(w, axis, -1)
    lead = w.shape[:-1]
    w = w.reshape(lead + (ML_HEADS, ML_HEAD_DIM))
    w = jnp.pad(w, [(0, 0)] * len(lead) + [(0, 0), (0, LANES - ML_HEAD_DIM)])
    return jnp.moveaxis(w.reshape(lead + (ML_PAD,)), -1, axis)


def _layer_params(l, norm1_g, w_in, da_lambda, da_subln_g, ml_conv_w, ml_conv_b, ml_wq, ml_wk, ml_gate_b,
                  ml_norm_g, ml_skip, cm_norm_g, cm_ws, cm_b, w_out, norm2_g, peer_wq, peer_keys, peer_u, peer_v):
    o = np.cumsum((0, 512, 512, 512, 256, 256, 256, 4, 4, 256, 256))
    seg = lambda i: w_in[:, o[i]:o[i + 1]]
    gates = jnp.pad(jnp.concatenate([seg(6), seg(7)], axis=1), ((0, 0), (0, LANES - 2 * ML_HEADS)))
    w_pad = jnp.concatenate([seg(0), seg(1), seg(2), _pad_heads(seg(3)), _pad_heads(seg(4)), _pad_heads(seg(5)),
                             seg(8), seg(9), gates], axis=1).astype(BF16)
    pad_sq = lambda w: jnp.pad(w, ((0, 0), (0, LANES - ML_HEAD_DIM), (0, LANES - ML_HEAD_DIM)))
    return dict(
        lam_init=0.8 - 0.6 * math.exp(-0.3 * l),
        g1=norm1_g.reshape(1, D_MODEL), w_pad=w_pad,
        lam_p=da_lambda, subln_g=da_subln_g.reshape(1, DA_VDIM),
        cw=_pad_heads(ml_conv_w), cb=_pad_heads(ml_conv_b.reshape(1, ML_WIDTH)),
        wq=pad_sq(ml_wq), wk=pad_sq(ml_wk), gate_b=ml_gate_b,
        ng=_pad_heads(ml_norm_g.reshape(1, ML_WIDTH)), sk=_pad_heads(ml_skip.reshape(1, ML_WIDTH)),
        cmg=cm_norm_g.reshape(1, CM_WIDTH), ws=cm_ws,
        bmat=jnp.repeat(cm_b.T, CM_GROUP_DIM, axis=1),
        wda=w_out[:DA_WIDTH].astype(BF16),
        wml=_pad_heads(w_out[DA_WIDTH:DA_WIDTH + ML_WIDTH], axis=0).astype(BF16),
        wcm=w_out[DA_WIDTH + ML_WIDTH:].astype(BF16),
        g2=norm2_g.reshape(1, D_MODEL), pwq=peer_wq.astype(BF16), pkeys=peer_keys.astype(BF16),
        u_bf=peer_u.astype(BF16), vt_bf=peer_v.T.astype(BF16),
    )


def _pick(n, prefs):
    for p in prefs:
        if n % p == 0:
            return p
    return n


def _layer(x, past, p, bias, final_g, final_norm):
    B, L, _ = x.shape
    T = B * L
    x2d = x.reshape(T, D_MODEL)
    tm = _pick(T, (512, 256, 128))
    q, kf, vf, kb, vb, mc, mv, mo, cu, cv, gc = _in_proj(x2d, p["g1"], p["w_pad"], tm)
    r3 = lambda a: a.reshape(B, L, a.shape[-1])
    if past is None:
        tq = _pick(L, (256,))
        oda = _attn_prompt(r3(q), r3(kb), r3(vb), bias, p["lam_p"], p["subln_g"], p["lam_init"], tq)
        c0 = jnp.zeros((B, ML_HEADS, LANES, LANES), F32)
        n0 = jnp.zeros((B, ML_HEADS, 1, LANES), F32)
        m0 = jnp.zeros((B, ML_HEADS, 1, LANES), F32)
        cv0 = jnp.zeros((B, ML_CONV - 1, ML_PAD), F32)
        lc = CM_CHUNK
    else:
        pk, pv, pc, pn, pm, pconv = past
        P = pk.shape[1]
        oda = _attn_step(r3(q), r3(kb), r3(vb), pk.reshape(B, P, DA_WIDTH), pv.reshape(B, P, DA_WIDTH),
                         bias[0], bias[1], p["lam_p"], p["subln_g"], p["lam_init"])
        c0 = jnp.pad(pc, ((0, 0), (0, 0), (0, LANES - ML_HEAD_DIM), (0, LANES - ML_HEAD_DIM)))
        n0 = jnp.pad(pn, ((0, 0), (0, 0), (0, LANES - ML_HEAD_DIM)))[:, :, None, :]
        m0 = jnp.broadcast_to(pm[:, :, None, None], (B, ML_HEADS, 1, LANES))
        cv0 = _pad_heads(pconv)
        lc = L
    oml, c_new, n_new, m_new, conv_new = _mlstm(
        p["gate_b"], r3(mc), r3(mv), r3(mo), r3(gc), p["cw"], p["cb"], p["wq"], p["wk"], p["ng"], p["sk"],
        c0, n0, m0, cv0)
    x1, vcm = _mix_out(x2d, oda.reshape(T, DA_WIDTH), oml.reshape(T, ML_PAD), cu, cv, p["cmg"], p["ws"], p["bmat"],
                       p["wda"], p["wml"], p["wcm"], _pick(T, (256, 128)), lc)
    xn2, st = _peer_query(x1, p["g2"], p["pwq"], p["pkeys"], _pick(T, (256, 128)))
    d, a, b = _peer_select(st, _pick(T, (128,)))
    x2 = _peer_dense(x1, xn2, st, d, a, b, p["u_bf"], p["vt_bf"], final_g, _pick(T, (512, 256, 128)), 512, final_norm)
    state = (kf.reshape(B, L, DA_HEADS, 2 * DA_HEAD_DIM), vf.reshape(B, L, DA_HEADS, DA_VDIM),
             c_new[:, :, :ML_HEAD_DIM, :ML_HEAD_DIM], n_new[:, :, 0, :ML_HEAD_DIM], m_new[:, :, 0, 0],
             conv_new.reshape(B, ML_CONV - 1, ML_HEADS, LANES)[..., :ML_HEAD_DIM].reshape(B, ML_CONV - 1, ML_WIDTH),
             vcm.reshape(B, L, CM_WIDTH))
    return x2.reshape(B, L, D_MODEL), state


def kernel(x_prompt, x_sample, cache_k, cache_v, state_mlstm_c, state_mlstm_n, state_mlstm_m, state_mlstm_conv, norm1_g, w_in, da_lambda, da_subln_g, rel_bias_table, ml_conv_w, ml_conv_b, ml_wq, ml_wk, ml_gate_b, ml_norm_g, ml_skip, cm_norm_g, cm_ws, cm_b, w_out, norm2_g, peer_wq, peer_keys, peer_u, peer_v, final_g):
    depth = w_in.shape[0]
    S = x_prompt.shape[1]
    Ld = x_sample.shape[1]
    P = cache_k.shape[2]
    tq = _pick(S, (256,))
    far_bucket = int(_rel_bucket_np(np.array(-(tq + 1))))
    ii = np.arange(tq)
    idx_prompt = np.stack([_rel_bucket_np(ii[None, :] - ii[:, None]),
                           _rel_bucket_np(ii[None, :] - tq - ii[:, None])])
    bias_prompt = _bias_tiles(rel_bias_table, idx_prompt, far_bucket)
    qpos = P + np.arange(Ld)
    bias_past = _bias_tiles(rel_bias_table, _rel_bucket_np(np.arange(P)[None, :] - qpos[:, None])[None], None)
    bias_new = _bias_tiles(rel_bias_table, _rel_bucket_np(qpos[None, :] - qpos[:, None])[None], None)
    fg = final_g.reshape(1, D_MODEL)

    hp, hs = x_prompt, x_sample
    st_p, st_s = [], []
    for l in range(depth):
        p = _layer_params(l, norm1_g[l], w_in[l], da_lambda[l], da_subln_g[l], ml_conv_w[l], ml_conv_b[l],
                          ml_wq[l], ml_wk[l], ml_gate_b[l], ml_norm_g[l], ml_skip[l], cm_norm_g[l], cm_ws[l],
                          cm_b[l], w_out[l], norm2_g[l], peer_wq[l], peer_keys[l], peer_u[l], peer_v[l])
        last = l == depth - 1
        hp, sp = _layer(hp, None, p, bias_prompt, fg, last)
        past = (cache_k[l], cache_v[l], state_mlstm_c[l], state_mlstm_n[l], state_mlstm_m[l], state_mlstm_conv[l])
        hs, ss = _layer(hs, past, p, (bias_past, bias_new), fg, last)
        st_p.append(sp)
        st_s.append(ss)
    stack = lambda sts, i: jnp.stack([s[i] for s in sts])
    return (hp, hs,
            stack(st_p, 0), stack(st_p, 1), stack(st_p, 2), stack(st_p, 3), stack(st_p, 4), stack(st_p, 5),
            stack(st_s, 0), stack(st_s, 1), stack(st_s, 2), stack(st_s, 3), stack(st_s, 4), stack(st_s, 5),
            stack(st_s, 6))
```

```python
import functools
import math

import numpy as np
import jax
import jax.numpy as jnp
from jax import lax
from jax.experimental import pallas as pl
from jax.experimental.pallas import tpu as pltpu

F32 = jnp.float32
BF16 = jnp.bfloat16

D_MODEL = 1024
CHUNK = 64
DA_HEADS = 4
DA_HEAD_DIM = 64
DA_VDIM = 128
DA_WIDTH = 512
ML_HEADS = 4
ML_HEAD_DIM = 64
ML_WIDTH = 256
ML_CONV = 4
CM_GROUPS = 4
CM_WIDTH = 256
CM_GROUP_DIM = 64
CM_CHUNK = 128
PEER_HEADS = 8
PEER_KEYS = 128
PEER_EXPERTS = PEER_KEYS * PEER_KEYS
PEER_QDIM = 256
PEER_HALF = 128
PEER_TOPK = 16
REL_BUCKETS = 32
REL_MAX_DIST = 128
EPS = 1e-6
NEG_INF = -1e30

LANES = 128
SUBLANES = 8
MXU_N = 256
ML_PAD = ML_HEADS * LANES
VMEM_LIMIT = 48 * 1024 * 1024

_C_Q, _C_K, _C_V = 0, 512, 1024
_C_MC, _C_MV, _C_MO = 1536, 2048, 2560
_C_CU, _C_CV, _C_G = 3072, 3328, 3584
IN_PAD = 3712


def _cparams(sem):
    return pltpu.CompilerParams(dimension_semantics=sem, vmem_limit_bytes=VMEM_LIMIT)


def _gelu(x):
    return 0.5 * x * (1.0 + lax.erf(x * (2.0 ** -0.5)))


def _dot_nt(a, b):
    return lax.dot_general(a, b, (((1,), (1,)), ((), ())), preferred_element_type=F32)


def _dot_tn(a, b):
    return lax.dot_general(a, b, (((0,), (0,)), ((), ())), preferred_element_type=F32)


def _inproj_kernel(x_ref, g_ref, w_ref, q_ref, kf_ref, vf_ref, kb_ref, vb_ref,
                   mc_ref, mv_ref, mo_ref, cu_ref, cv_ref, gc_ref):
    x = x_ref[...]
    xn = x * lax.rsqrt(jnp.mean(x * x, axis=-1, keepdims=True) + EPS) * g_ref[...]
    xb = xn.astype(BF16)

    def proj(lo, hi):
        return jnp.dot(xb, w_ref[:, lo:hi], preferred_element_type=F32)

    q_ref[...] = (proj(_C_Q, _C_K) * (DA_HEAD_DIM ** -0.5)).astype(BF16)
    k = proj(_C_K, _C_V)
    kf_ref[...] = k
    kb_ref[...] = k.astype(BF16)
    v = proj(_C_V, _C_MC)
    vf_ref[...] = v
    vb_ref[...] = v.astype(BF16)
    mc_ref[...] = proj(_C_MC, _C_MV)
    mv_ref[...] = proj(_C_MV, _C_MO)
    mo_ref[...] = proj(_C_MO, _C_CU)
    cu_ref[...] = proj(_C_CU, _C_CV)
    cv_ref[...] = proj(_C_CV, _C_G)
    gc_ref[...] = proj(_C_G, IN_PAD)


def _in_proj(x2d, g, w_pad, tm):
    T = x2d.shape[0]
    row = lambda n: pl.BlockSpec((tm, n), lambda i: (i, 0))
    full = lambda a: pl.BlockSpec(a.shape, lambda i: (0,) * a.ndim)
    outs = [(DA_WIDTH, BF16), (DA_WIDTH, F32), (DA_WIDTH, F32), (DA_WIDTH, BF16), (DA_WIDTH, BF16),
            (ML_PAD, F32), (ML_PAD, F32), (ML_PAD, F32), (CM_WIDTH, F32), (CM_WIDTH, F32), (LANES, F32)]
    return pl.pallas_call(
        _inproj_kernel,
        grid=(T // tm,),
        in_specs=[row(D_MODEL), full(g), full(w_pad)],
        out_specs=[row(n) for n, _ in outs],
        out_shape=[jax.ShapeDtypeStruct((T, n), dt) for n, dt in outs],
        compiler_params=_cparams(("parallel",)),
        name="in_proj",
    )(x2d, g, w_pad)


def _rel_bucket_np(rel):
    half = REL_BUCKETS // 2
    max_exact = half // 2
    ret = np.where(rel > 0, half, 0)
    n = np.abs(rel)
    nf = np.maximum(n, 1).astype(np.float32)
    large = max_exact + (np.log(nf / np.float32(max_exact)) / np.float32(math.log(REL_MAX_DIST / max_exact))
                         * np.float32(half - max_exact)).astype(np.int32)
    large = np.minimum(large, half - 1)
    return (ret + np.where(n < max_exact, n, large)).astype(np.int32)


def _bias_kernel(tab_ref, idx_ref, o_ref, *, shift_bucket):
    for n in range(idx_ref.shape[0]):
        idx = idx_ref[n]
        for h in range(DA_HEADS):
            acc = jnp.zeros(idx.shape, F32)
            for b in range(REL_BUCKETS):
                acc = jnp.where(idx == b, tab_ref[b, h], acc)
            if shift_bucket is not None:
                acc = acc - tab_ref[shift_bucket, h]
            o_ref[n, h] = acc


def _bias_tiles(table, idx_np, shift_bucket):
    n, r, c = idx_np.shape
    return pl.pallas_call(
        functools.partial(_bias_kernel, shift_bucket=shift_bucket),
        in_specs=[pl.BlockSpec(memory_space=pltpu.SMEM), pl.BlockSpec(memory_space=pltpu.VMEM)],
        out_specs=pl.BlockSpec(memory_space=pltpu.VMEM),
        out_shape=jax.ShapeDtypeStruct((n, DA_HEADS, r, c), F32),
        name="rel_bias",
    )(table, jnp.asarray(idx_np))


def _lambda(lam_ref, lam_init):
    lp = lam_ref[...]
    a = jnp.sum(lp[0:1] * lp[1:2], axis=-1, keepdims=True)
    b = jnp.sum(lp[2:3] * lp[3:4], axis=-1, keepdims=True)
    return jnp.exp(a) - jnp.exp(b) + lam_init


def _subln(o1, l1, o2, l2, lam, g, lam_init):
    o = o1 / l1 - lam * (o2 / l2)
    o = o * lax.rsqrt(jnp.mean(o * o, axis=-1, keepdims=True) + EPS) * g
    return o * (1.0 - lam_init)


def _attn_prompt_kernel(lam_ref, q_ref, k_ref, v_ref, bias_ref, g_ref, o_ref, *, tq, lam_init):
    qb = pl.program_id(1)
    lam = _lambda(lam_ref, lam_init)
    first = lax.broadcasted_iota(jnp.int32, (1, LANES), 1) < DA_HEAD_DIM
    rows = lax.broadcasted_iota(jnp.int32, (tq, tq), 0)
    cols = lax.broadcasted_iota(jnp.int32, (tq, tq), 1)
    diag_mask = (cols // CHUNK) <= (rows // CHUNK)
    has_prev = qb >= 1
    prev_off = pl.multiple_of(jnp.maximum(qb - 1, 0) * tq, tq)
    diag_off = pl.multiple_of(qb * tq, tq)
    g = g_ref[...]

    for h in range(DA_HEADS):
        sl = slice(LANES * h, LANES * (h + 1))
        q = q_ref[0, :, sl]
        res = []
        for c in range(2):
            qc = jnp.where(first if c == 0 else jnp.logical_not(first), q, jnp.zeros_like(q))

            def blk(koff, carry, bias=None, mask=None, qc=qc, sl=sl):
                m, l, acc = carry
                k = k_ref[0, pl.ds(koff, tq), sl]
                v = v_ref[0, pl.ds(koff, tq), sl]
                s = _dot_nt(qc, k)
                if bias is not None:
                    s = s + bias
                if mask is not None:
                    s = jnp.where(mask, s, NEG_INF)
                m_new = jnp.maximum(m, jnp.max(s, axis=-1, keepdims=True))
                alpha = jnp.exp(m - m_new)
                p = jnp.exp(s - m_new)
                l = alpha * l + jnp.sum(p, axis=-1, keepdims=True)
                acc = alpha * acc + jnp.dot(p.astype(BF16), v, preferred_element_type=F32)
                return m_new, l, acc

            carry = (jnp.full((tq, 1), NEG_INF, F32), jnp.zeros((tq, 1), F32), jnp.zeros((tq, LANES), F32))
            carry = blk(diag_off, carry, bias_ref[0, h], diag_mask)
            carry = blk(prev_off, carry, bias_ref[1, h], has_prev)
            carry = lax.fori_loop(
                0, jnp.maximum(qb - 1, 0),
                lambda j, cr: blk(pl.multiple_of(j * tq, tq), cr), carry)
            res.append(carry)
        (_, l1, a1), (_, l2, a2) = res
        o_ref[0, :, sl] = _subln(a1, l1, a2, l2, lam, g, lam_init).astype(o_ref.dtype)


def _attn_prompt(q, k, v, bias, lam_p, g, lam_init, tq):
    B, S, _ = q.shape
    return pl.pallas_call(
        functools.partial(_attn_prompt_kernel, tq=tq, lam_init=lam_init),
        grid=(B, S // tq),
        in_specs=[
            pl.BlockSpec(lam_p.shape, lambda b, i: (0, 0)),
            pl.BlockSpec((1, tq, DA_WIDTH), lambda b, i: (b, i, 0)),
            pl.BlockSpec((1, S, DA_WIDTH), lambda b, i: (b, 0, 0)),
            pl.BlockSpec((1, S, DA_WIDTH), lambda b, i: (b, 0, 0)),
            pl.BlockSpec(bias.shape, lambda b, i: (0, 0, 0, 0)),
            pl.BlockSpec(g.shape, lambda b, i: (0, 0)),
        ],
        out_specs=pl.BlockSpec((1, tq, DA_WIDTH), lambda b, i: (b, i, 0)),
        out_shape=jax.ShapeDtypeStruct((B, S, DA_WIDTH), BF16),
        compiler_params=_cparams(("parallel", "arbitrary")),
        name="attn_prompt",
    )(lam_p, q, k, v, bias, g)


def _attn_step_kernel(lam_ref, q_ref, kn_ref, vn_ref, kp_ref, vp_ref, bp_ref, bn_ref, g_ref, o_ref, *, lam_init):
    lam = _lambda(lam_ref, lam_init)
    first = lax.broadcasted_iota(jnp.int32, (1, LANES), 1) < DA_HEAD_DIM
    g = g_ref[...]
    for h in range(DA_HEADS):
        sl = slice(LANES * h, LANES * (h + 1))
        q = q_ref[0, :, sl]
        kn = kn_ref[0, :, sl]
        vn = vn_ref[0, :, sl]
        kp = kp_ref[0, :, sl].astype(BF16)
        vp = vp_ref[0, :, sl].astype(BF16)
        res = []
        for c in range(2):
            qc = jnp.where(first if c == 0 else jnp.logical_not(first), q, jnp.zeros_like(q))
            sp = _dot_nt(qc, kp) + bp_ref[0, h]
            sn = _dot_nt(qc, kn) + bn_ref[0, h]
            m = jnp.maximum(jnp.max(sp, axis=-1, keepdims=True), jnp.max(sn, axis=-1, keepdims=True))
            pp = jnp.exp(sp - m)
            pn = jnp.exp(sn - m)
            l = jnp.sum(pp, axis=-1, keepdims=True) + jnp.sum(pn, axis=-1, keepdims=True)
            acc = (jnp.dot(pp.astype(BF16), vp, preferred_element_type=F32)
                   + jnp.dot(pn.astype(BF16), vn, preferred_element_type=F32))
            res.append((l, acc))
        (l1, a1), (l2, a2) = res
        o_ref[0, :, sl] = _subln(a1, l1, a2, l2, lam, g, lam_init).astype(o_ref.dtype)


def _attn_step(q, kn, vn, kp, vp, bias_p, bias_n, lam_p, g, lam_init):
    B, L, _ = q.shape
    P = kp.shape[1]
    c3 = lambda b: (b, 0, 0)
    z4 = lambda b: (0, 0, 0, 0)
    return pl.pallas_call(
        functools.partial(_attn_step_kernel, lam_init=lam_init),
        grid=(B,),
        in_specs=[
            pl.BlockSpec(lam_p.shape, lambda b: (0, 0)),
            pl.BlockSpec((1, L, DA_WIDTH), c3), pl.BlockSpec((1, L, DA_WIDTH), c3), pl.BlockSpec((1, L, DA_WIDTH), c3),
            pl.BlockSpec((1, P, DA_WIDTH), c3), pl.BlockSpec((1, P, DA_WIDTH), c3),
            pl.BlockSpec(bias_p.shape, z4), pl.BlockSpec(bias_n.shape, z4),
            pl.BlockSpec(g.shape, lambda b: (0, 0)),
        ],
        out_specs=pl.BlockSpec((1, L, DA_WIDTH), c3),
        out_shape=jax.ShapeDtypeStruct((B, L, DA_WIDTH), BF16),
        compiler_params=_cparams(("parallel",)),
        name="attn_step",
    )(lam_p, q, kn, vn, kp, vp, bias_p, bias_n, g)


def _log_sigmoid(x):
    return jnp.minimum(x, 0.0) - jnp.log(1.0 + jnp.exp(-jnp.abs(x)))


def _mlstm_kernel(gb_ref, mc_ref, mv_ref, mo_ref, gc_ref, cw_ref, cb_ref, wq_ref, wk_ref, ng_ref, sk_ref,
                  c0_ref, n0_ref, m0_ref, cv0_ref,
                  o_ref, co_ref, no_ref, mo2_ref, cvo_ref,
                  xbuf, c_sc, n_sc, m_sc):
    j = pl.program_id(1)
    L = CHUNK

    @pl.when(j == 0)
    def _():
        xbuf[5:8, :] = cv0_ref[0]
        c_sc[...] = c0_ref[0]
        n_sc[...] = n0_ref[0]
        m_sc[...] = m0_ref[0]

    x = mc_ref[0]
    xbuf[8:8 + L, :] = x
    y = (cb_ref[...] + cw_ref[3:4, :] * x + cw_ref[2:3, :] * xbuf[7:7 + L, :]
         + cw_ref[1:2, :] * xbuf[6:6 + L, :] + cw_ref[0:1, :] * xbuf[5:5 + L, :])
    cc = y * jax.nn.sigmoid(y)
    tail = xbuf[5 + L:8 + L, :]
    xbuf[5:8, :] = tail
    cvo_ref[0] = tail

    gcol = gc_ref[0]
    gt = jnp.concatenate([gcol, jnp.zeros((LANES - L, LANES), F32)], axis=0).T
    r = lax.broadcasted_iota(jnp.int32, (L, L), 0)
    s = lax.broadcasted_iota(jnp.int32, (L, L), 1)
    tril = s <= r

    for h in range(ML_HEADS):
        sl = slice(LANES * h, LANES * (h + 1))
        b_i = gb_ref[0, h]
        b_f = gb_ref[1, h]
        ig_c = gcol[:, h:h + 1] + b_i
        lf_c = _log_sigmoid(gcol[:, 4 + h:5 + h] + b_f)
        ig_r = gt[h:h + 1, 0:L] + b_i
        lf_r = _log_sigmoid(gt[4 + h:5 + h, 0:L] + b_f)
        f_c = jnp.sum(jnp.where(tril, lf_r, 0.0), axis=1, keepdims=True)
        f_r = jnp.sum(jnp.where(r <= s, lf_c, 0.0), axis=0, keepdims=True)
        m_prev = m_sc[h][0:1, 0:1]
        dmat = jnp.where(tril, f_c - f_r + ig_r, NEG_INF)
        inter = f_c + m_prev
        m_t = jnp.maximum(inter, jnp.max(dmat, axis=1, keepdims=True))
        cch = cc[:, sl]
        qh = jnp.dot(cch, wq_ref[h], preferred_element_type=F32)
        kh = jnp.dot(cch, wk_ref[h], preferred_element_type=F32) * (ML_HEAD_DIM ** -0.5)
        vh = mv_ref[0, :, sl]
        sw = _dot_nt(qh, kh) * jnp.exp(dmat - m_t)
        iw = jnp.exp(inter - m_t)
        c_prev = c_sc[h]
        n_prev = n_sc[h]
        num = jnp.dot(sw, vh, preferred_element_type=F32) + iw * jnp.dot(qh, c_prev, preferred_element_type=F32)
        den = jnp.sum(sw, axis=1, keepdims=True) + iw * jnp.sum(qh * n_prev, axis=1, keepdims=True)
        hout = num / jnp.maximum(jnp.abs(den), jnp.exp(-m_t))
        fl = f_c[L - 1:L, :]
        tail_r = fl - f_r + ig_r
        tail_c = fl - f_c + ig_c
        m_new = jnp.maximum(fl + m_prev, jnp.max(tail_r, axis=1, keepdims=True))
        wc = jnp.exp(fl + m_prev - m_new)
        kw = kh * jnp.exp(tail_c - m_new)
        c_new = wc * c_prev + _dot_tn(kw, vh)
        n_new = wc * n_prev + jnp.sum(kw, axis=0, keepdims=True)
        c_sc[h] = c_new
        n_sc[h] = n_new
        m_sc[h] = jnp.broadcast_to(m_new, (1, LANES))
        co_ref[0, h] = c_new
        no_ref[0, h] = n_new
        mo2_ref[0, h] = jnp.broadcast_to(m_new, (1, LANES))
        hn = hout * lax.rsqrt(jnp.sum(hout * hout, axis=1, keepdims=True) * (1.0 / ML_HEAD_DIM) + EPS) * ng_ref[:, sl]
        o_ref[0, :, sl] = ((hn + sk_ref[:, sl] * cch) * jax.nn.sigmoid(mo_ref[0, :, sl])).astype(o_ref.dtype)


def _mlstm(gate_b, mc, mv, mo, gc, cw, cb, wq, wk, ng, sk, c0, n0, m0, cv0):
    B, L, _ = mc.shape
    nc = L // CHUNK
    tok = lambda n: pl.BlockSpec((1, CHUNK, n), lambda b, j: (b, j, 0))
    full = lambda a: pl.BlockSpec(a.shape, lambda b, j: (0,) * a.ndim)
    st4 = lambda a: pl.BlockSpec((1,) + a.shape[1:], lambda b, j: (b,) + (0,) * (a.ndim - 1))
    return pl.pallas_call(
        _mlstm_kernel,
        grid=(B, nc),
        in_specs=[pl.BlockSpec(memory_space=pltpu.SMEM),
                  tok(ML_PAD), tok(ML_PAD), tok(ML_PAD), tok(LANES),
                  full(cw), full(cb), full(wq), full(wk), full(ng), full(sk),
                  st4(c0), st4(n0), st4(m0), st4(cv0)],
        out_specs=[tok(ML_PAD), st4(c0), st4(n0), st4(m0), st4(cv0)],
        out_shape=[jax.ShapeDtypeStruct((B, L, ML_PAD), BF16),
                   jax.ShapeDtypeStruct(c0.shape, F32), jax.ShapeDtypeStruct(n0.shape, F32),
                   jax.ShapeDtypeStruct(m0.shape, F32), jax.ShapeDtypeStruct(cv0.shape, F32)],
        scratch_shapes=[pltpu.VMEM((8 + CHUNK, ML_PAD), F32),
                        pltpu.VMEM((ML_HEADS, LANES, LANES), F32),
                        pltpu.VMEM((ML_HEADS, 1, LANES), F32),
                        pltpu.VMEM((ML_HEADS, 1, LANES), F32)],
        compiler_params=_cparams(("parallel", "arbitrary")),
        name="mlstm",
    )(gate_b, mc, mv, mo, gc, cw, cb, wq, wk, ng, sk, c0, n0, m0, cv0)


def _mixout_kernel(x_ref, oda_ref, oml_ref, cu_ref, cv_ref, cmg_ref, ws_ref, bm_ref,
                   wda_ref, wml_ref, wcm_ref, x1_ref, vcm_ref, *, lc):
    tm = x_ref.shape[0]
    u = _gelu(cu_ref[...])
    gv = _gelu(cv_ref[...])
    vcm = gv * lax.rsqrt(jnp.mean(gv * gv, axis=-1, keepdims=True) + EPS) * cmg_ref[...]
    vcm_ref[...] = vcm
    r = lax.broadcasted_iota(jnp.int32, (lc, lc), 0)
    c = lax.broadcasted_iota(jnp.int32, (lc, lc), 1)
    grp = lax.broadcasted_iota(jnp.int32, (1, CM_WIDTH), 1) // CM_GROUP_DIM
    wsm = [jnp.where(c <= r, ws_ref[g, 0:lc, 0:lc], 0.0).astype(BF16) for g in range(CM_GROUPS)]
    pieces = []
    for ci in range(tm // lc):
        vch = vcm[ci * lc:(ci + 1) * lc].astype(BF16)
        mixed = bm_ref[0:lc, :]
        for g in range(CM_GROUPS):
            mixed = mixed + jnp.where(grp == g, jnp.dot(wsm[g], vch, preferred_element_type=F32), 0.0)
        pieces.append(u[ci * lc:(ci + 1) * lc] * mixed)
    ocm = jnp.concatenate(pieces, axis=0) if len(pieces) > 1 else pieces[0]
    y = (jnp.dot(oda_ref[...], wda_ref[...], preferred_element_type=F32)
         + jnp.dot(oml_ref[...], wml_ref[...], preferred_element_type=F32)
         + jnp.dot(ocm.astype(BF16), wcm_ref[...], preferred_element_type=F32))
    x1_ref[...] = x_ref[...] + y


def _mix_out(x2d, oda, oml, cu, cv, cmg, ws, bmat, wda, wml, wcm, tm, lc):
    T = x2d.shape[0]
    row = lambda n: pl.BlockSpec((tm, n), lambda i: (i, 0))
    full = lambda a: pl.BlockSpec(a.shape, lambda i: (0,) * a.ndim)
    return pl.pallas_call(
        functools.partial(_mixout_kernel, lc=lc),
        grid=(T // tm,),
        in_specs=[row(D_MODEL), row(DA_WIDTH), row(ML_PAD), row(CM_WIDTH), row(CM_WIDTH),
                  full(cmg), full(ws), full(bmat), full(wda), full(wml), full(wcm)],
        out_specs=[row(D_MODEL), row(CM_WIDTH)],
        out_shape=[jax.ShapeDtypeStruct((T, D_MODEL), F32), jax.ShapeDtypeStruct((T, CM_WIDTH), F32)],
        compiler_params=_cparams(("parallel",)),
        name="mix_out",
    )(x2d, oda, oml, cu, cv, cmg, ws, bmat, wda, wml, wcm)


def _peer_query_kernel(x_ref, g_ref, wq_ref, keys_ref, xnt_ref, st_ref):
    x = x_ref[...]
    xf = x * lax.rsqrt(jnp.mean(x * x, axis=-1, keepdims=True) + EPS) * g_ref[...]
    xn = xf.astype(BF16)
    xnt_ref[...] = xf.T.astype(BF16)
    for h in range(PEER_HEADS):
        q = jnp.dot(xn, wq_ref[:, PEER_QDIM * h:PEER_QDIM * (h + 1)], preferred_element_type=F32)
        q = q * lax.rsqrt(jnp.mean(q * q, axis=-1, keepdims=True) + EPS)
        for c in range(2):
            qc = q[:, PEER_HALF * c:PEER_HALF * (c + 1)].astype(BF16)
            st_ref[h, c] = _dot_nt(keys_ref[h, c], qc)


def _peer_query(x1, g, wq, keys, tm):
    T = x1.shape[0]
    full = lambda a: pl.BlockSpec(a.shape, lambda i: (0,) * a.ndim)
    return pl.pallas_call(
        _peer_query_kernel,
        grid=(T // tm,),
        in_specs=[pl.BlockSpec((tm, D_MODEL), lambda i: (i, 0)), full(g), full(wq), full(keys)],
        out_specs=[pl.BlockSpec((D_MODEL, tm), lambda i: (0, i)),
                   pl.BlockSpec((PEER_HEADS, 2, PEER_KEYS, tm), lambda i: (0, 0, 0, i))],
        out_shape=[jax.ShapeDtypeStruct((D_MODEL, T), BF16),
                   jax.ShapeDtypeStruct((PEER_HEADS, 2, PEER_KEYS, T), F32)],
        compiler_params=_cparams(("parallel",)),
        name="peer_query",
    )(x1, g, wq, keys)


_PAIRS = [(a, b) for a in range(PEER_TOPK) for b in range(PEER_TOPK) if (a + 1) * (b + 1) <= PEER_TOPK]
_NCAND = ((len(_PAIRS) + 7) // 8) * 8


def _peer_select_kernel(st_ref, d_ref, a_ref, b_ref, vals, cand):
    tt = st_ref.shape[-1]
    for h in range(PEER_HEADS):
        for c in range(2):
            cur = st_ref[h, c]
            for i in range(PEER_TOPK):
                m = jnp.max(cur, axis=0, keepdims=True)
                vals[c, i:i + 1, :] = m
                cur = jnp.where(cur == m, NEG_INF, cur)
        cand[...] = jnp.full((_NCAND, tt), NEG_INF, F32)
        for i, (a, b) in enumerate(_PAIRS):
            cand[i:i + 1, :] = vals[0, a:a + 1, :] + vals[1, b:b + 1, :]
        top1 = vals[0, 0:1, :]
        top2 = vals[1, 0:1, :]
        best = top1 + top2
        cur = cand[...]
        z = jnp.zeros((1, tt), F32)
        thr = best
        for i in range(PEER_TOPK):
            thr = jnp.max(cur, axis=0, keepdims=True)
            z = z + jnp.exp(thr - best)
            cur = jnp.where(cur == thr, NEG_INF, cur)
        s1 = st_ref[h, 0]
        d_ref[h] = thr - s1
        a_ref[h] = 0.5 * jnp.exp(s1 - top1) / z
        b_ref[h] = jnp.exp(st_ref[h, 1] - top2)


def _peer_select(st, tt):
    T = st.shape[-1]
    spec = pl.BlockSpec((PEER_HEADS, PEER_KEYS, tt), lambda i: (0, 0, i))
    shp = jax.ShapeDtypeStruct((PEER_HEADS, PEER_KEYS, T), F32)
    return pl.pallas_call(
        _peer_select_kernel,
        grid=(T // tt,),
        in_specs=[pl.BlockSpec((PEER_HEADS, 2, PEER_KEYS, tt), lambda i: (0, 0, 0, i))],
        out_specs=[spec, spec, spec],
        out_shape=[shp, shp, shp],
        scratch_shapes=[pltpu.VMEM((2, PEER_TOPK, tt), F32), pltpu.VMEM((_NCAND, tt), F32)],
        compiler_params=_cparams(("parallel",)),
        name="peer_select",
    )(st)


def _peer_dense_kernel(x1_ref, xnt_ref, s2_ref, d_ref, a_ref, b_ref, u_ref, vt_ref, fg_ref, o_ref,
                       acc_ref, ht_ref, at_ref, *, rows, final_norm):
    j = pl.program_id(1)
    last = pl.num_programs(1) - 1

    @pl.when(j == 0)
    def _():
        acc_ref[...] = jnp.zeros_like(acc_ref)

    ht_ref[...] = jnp.dot(u_ref[...], xnt_ref[...], preferred_element_type=F32)
    per = MXU_N // PEER_KEYS
    tt = ht_ref.shape[1]
    lw = LANES
    assert rows == SUBLANES
    i0 = pl.multiple_of(j * rows, SUBLANES)
    for c in range(rows // per):
        rr = range(c * per, (c + 1) * per)
        for lt in range(tt // lw):
            ls = slice(lt * lw, (lt + 1) * lw)
            dt = [d_ref[h, pl.ds(i0, rows), ls] for h in range(PEER_HEADS)]
            at_ = [a_ref[h, pl.ds(i0, rows), ls] for h in range(PEER_HEADS)]
            dv = [[jnp.broadcast_to(dt[h][r:r + 1], (SUBLANES, lw)) for h in range(PEER_HEADS)] for r in rr]
            av = [[jnp.broadcast_to(at_[h][r:r + 1], (SUBLANES, lw)) for h in range(PEER_HEADS)] for r in rr]
            for sg in range(PEER_KEYS // (2 * SUBLANES)):
                halves = [[] for _ in rr]
                for half in range(2):
                    k0 = sg * 2 * SUBLANES + half * SUBLANES
                    ks = slice(k0, k0 + SUBLANES)
                    s2v = [s2_ref[h, ks, ls] for h in range(PEER_HEADS)]
                    bv = [b_ref[h, ks, ls] for h in range(PEER_HEADS)]
                    for ri, r in enumerate(rr):
                        t = [jnp.where(s2v[h] >= dv[ri][h], bv[h], 0.0) * av[ri][h] for h in range(PEER_HEADS)]
                        w = ((t[0] + t[1]) + (t[2] + t[3])) + ((t[4] + t[5]) + (t[6] + t[7]))
                        ht = ht_ref[r * PEER_KEYS + k0:r * PEER_KEYS + k0 + SUBLANES, ls]
                        halves[ri].append(w * (ht * (1.0 + lax.erf(ht * (2.0 ** -0.5)))))
                for ri in range(per):
                    o0 = ri * PEER_KEYS + sg * 2 * SUBLANES
                    at_ref[c, o0:o0 + 2 * SUBLANES, ls] = jnp.concatenate(halves[ri], axis=0).astype(BF16)
        acc_ref[...] += jnp.dot(vt_ref[:, c * MXU_N:(c + 1) * MXU_N], at_ref[c], preferred_element_type=F32)

    @pl.when(j == last)
    def _():
        x2 = x1_ref[...] + acc_ref[...].T
        if final_norm:
            x2 = x2 * lax.rsqrt(jnp.mean(x2 * x2, axis=-1, keepdims=True) + EPS) * fg_ref[...]
        o_ref[...] = x2


def _peer_dense(x1, xn, st, d, a, b, u_bf, vt_bf, fg, tt, eb, final_norm):
    T = x1.shape[0]
    rows = eb // PEER_KEYS
    ne = PEER_EXPERTS // eb
    hk = pl.BlockSpec((PEER_HEADS, PEER_KEYS, tt), lambda i, j: (0, 0, i))
    return pl.pallas_call(
        functools.partial(_peer_dense_kernel, rows=rows, final_norm=final_norm),
        grid=(T // tt, ne),
        in_specs=[pl.BlockSpec((tt, D_MODEL), lambda i, j: (i, 0)),
                  pl.BlockSpec((D_MODEL, tt), lambda i, j: (0, i)),
                  pl.BlockSpec((PEER_HEADS, None, PEER_KEYS, tt), lambda i, j: (0, 1, 0, i)),
                  hk, hk, hk,
                  pl.BlockSpec((eb, D_MODEL), lambda i, j: (j, 0)),
                  pl.BlockSpec((D_MODEL, eb), lambda i, j: (0, j)),
                  pl.BlockSpec(fg.shape, lambda i, j: (0, 0))],
        out_specs=pl.BlockSpec((tt, D_MODEL), lambda i, j: (i, 0)),
        out_shape=jax.ShapeDtypeStruct((T, D_MODEL), F32),
        scratch_shapes=[pltpu.VMEM((D_MODEL, tt), F32), pltpu.VMEM((eb, tt), F32),
                        pltpu.VMEM((eb // MXU_N, MXU_N, tt), BF16)],
        compiler_params=_cparams(("parallel", "arbitrary")),
        name="peer_dense",
    )(x1, xn, st, d, a, b, u_bf, vt_bf, fg)


def _pad_heads(w, axis=-1):
    w = jnp.moveaxis(w, axis, -1)
    lead = w.shape[:-1]
    w = w.reshape(lead + (ML_HEADS, ML_HEAD_DIM))
    w = jnp.pad(w, [(0, 0)] * len(lead) + [(0, 0), (0, LANES - ML_HEAD_DIM)])
    return jnp.moveaxis(w.reshape(lead + (ML_PAD,)), -1, axis)


def _layer_params(l, norm1_g, w_in, da_lambda, da_subln_g, ml_conv_w, ml_conv_b, ml_wq, ml_wk, ml_gate_b,
                  ml_norm_g, ml_skip, cm_norm_g, cm_ws, cm_b, w_out, norm2_g, peer_wq, peer_keys, peer_u, peer_v):
    o = np.cumsum((0, 512, 512, 512, 256, 256, 256, 4, 4, 256, 256))
    seg = lambda i: w_in[:, o[i]:o[i + 1]]
    gates = jnp.pad(jnp.concatenate([seg(6), seg(7)], axis=1), ((0, 0), (0, LANES - 2 * ML_HEADS)))
    w_pad = jnp.concatenate([seg(0), seg(1), seg(2), _pad_heads(seg(3)), _pad_heads(seg(4)), _pad_heads(seg(5)),
                             seg(8), seg(9), gates], axis=1).astype(BF16)
    pad_sq = lambda w: jnp.pad(w, ((0, 0), (0, LANES - ML_HEAD_DIM), (0, LANES - ML_HEAD_DIM)))
    return dict(
        lam_init=0.8 - 0.6 * math.exp(-0.3 * l),
        g1=norm1_g.reshape(1, D_MODEL), w_pad=w_pad,
        lam_p=da_lambda, subln_g=da_subln_g.reshape(1, DA_VDIM),
        cw=_pad_heads(ml_conv_w), cb=_pad_heads(ml_conv_b.reshape(1, ML_WIDTH)),
        wq=pad_sq(ml_wq), wk=pad_sq(ml_wk), gate_b=ml_gate_b,
        ng=_pad_heads(ml_norm_g.reshape(1, ML_WIDTH)), sk=_pad_heads(ml_skip.reshape(1, ML_WIDTH)),
        cmg=cm_norm_g.reshape(1, CM_WIDTH), ws=cm_ws,
        bmat=jnp.repeat(cm_b.T, CM_GROUP_DIM, axis=1),
        wda=w_out[:DA_WIDTH].astype(BF16),
        wml=_pad_heads(w_out[DA_WIDTH:DA_WIDTH + ML_WIDTH], axis=0).astype(BF16),
        wcm=w_out[DA_WIDTH + ML_WIDTH:].astype(BF16),
        g2=norm2_g.reshape(1, D_MODEL), pwq=peer_wq.astype(BF16), pkeys=peer_keys.astype(BF16),
        u_bf=peer_u.astype(BF16), vt_bf=peer_v.T.astype(BF16),
    )


def _pick(n, prefs):
    for p in prefs:
        if n % p == 0:
            return p
    return n


def _layer(x, past, p, bias, final_g, final_norm):
    B, L, _ = x.shape
    T = B * L
    x2d = x.reshape(T, D_MODEL)
    tm = _pick(T, (512, 256, 128))
    q, kf, vf, kb, vb, mc, mv, mo, cu, cv, gc = _in_proj(x2d, p["g1"], p["w_pad"], tm)
    r3 = lambda a: a.reshape(B, L, a.shape[-1])
    if past is None:
        tq = _pick(L, (256,))
        oda = _attn_prompt(r3(q), r3(kb), r3(vb), bias, p["lam_p"], p["subln_g"], p["lam_init"], tq)
        c0 = jnp.zeros((B, ML_HEADS, LANES, LANES), F32)
        n0 = jnp.zeros((B, ML_HEADS, 1, LANES), F32)
        m0 = jnp.zeros((B, ML_HEADS, 1, LANES), F32)
        cv0 = jnp.zeros((B, ML_CONV - 1, ML_PAD), F32)
        lc = CM_CHUNK
    else:
        pk, pv, pc, pn, pm, pconv = past
        P = pk.shape[1]
        oda = _attn_step(r3(q), r3(kb), r3(vb), pk.reshape(B, P, DA_WIDTH), pv.reshape(B, P, DA_WIDTH),
                         bias[0], bias[1], p["lam_p"], p["subln_g"], p["lam_init"])
        c0 = jnp.pad(pc, ((0, 0), (0, 0), (0, LANES - ML_HEAD_DIM), (0, LANES - ML_HEAD_DIM)))
        n0 = jnp.pad(pn, ((0, 0), (0, 0), (0, LANES - ML_HEAD_DIM)))[:, :, None, :]
        m0 = jnp.broadcast_to(pm[:, :, None, None], (B, ML_HEADS, 1, LANES))
        cv0 = _pad_heads(pconv)
        lc = L
    oml, c_new, n_new, m_new, conv_new = _mlstm(
        p["gate_b"], r3(mc), r3(mv), r3(mo), r3(gc), p["cw"], p["cb"], p["wq"], p["wk"], p["ng"], p["sk"],
        c0, n0, m0, cv0)
    x1, vcm = _mix_out(x2d, oda.reshape(T, DA_WIDTH), oml.reshape(T, ML_PAD), cu, cv, p["cmg"], p["ws"], p["bmat"],
                       p["wda"], p["wml"], p["wcm"], _pick(T, (256, 128)), lc)
    xn2, st = _peer_query(x1, p["g2"], p["pwq"], p["pkeys"], _pick(T, (256, 128)))
    d, a, b = _peer_select(st, _pick(T, (128,)))
    x2 = _peer_dense(x1, xn2, st, d, a, b, p["u_bf"], p["vt_bf"], final_g, _pick(T, (512, 256, 128)), 1024, final_norm)
    state = (kf.reshape(B, L, DA_HEADS, 2 * DA_HEAD_DIM), vf.reshape(B, L, DA_HEADS, DA_VDIM),
             c_new[:, :, :ML_HEAD_DIM, :ML_HEAD_DIM], n_new[:, :, 0, :ML_HEAD_DIM], m_new[:, :, 0, 0],
             conv_new.reshape(B, ML_CONV - 1, ML_HEADS, LANES)[..., :ML_HEAD_DIM].reshape(B, ML_CONV - 1, ML_WIDTH),
             vcm.reshape(B, L, CM_WIDTH))
    return x2.reshape(B, L, D_MODEL), state


def kernel(x_prompt, x_sample, cache_k, cache_v, state_mlstm_c, state_mlstm_n, state_mlstm_m, state_mlstm_conv, norm1_g, w_in, da_lambda, da_subln_g, rel_bias_table, ml_conv_w, ml_conv_b, ml_wq, ml_wk, ml_gate_b, ml_norm_g, ml_skip, cm_norm_g, cm_ws, cm_b, w_out, norm2_g, peer_wq, peer_keys, peer_u, peer_v, final_g):
    depth = w_in.shape[0]
    S = x_prompt.shape[1]
    Ld = x_sample.shape[1]
    P = cache_k.shape[2]
    tq = _pick(S, (256,))
    far_bucket = int(_rel_bucket_np(np.array(-(tq + 1))))
    ii = np.arange(tq)
    idx_prompt = np.stack([_rel_bucket_np(ii[None, :] - ii[:, None]),
                           _rel_bucket_np(ii[None, :] - tq - ii[:, None])])
    bias_prompt = _bias_tiles(rel_bias_table, idx_prompt, far_bucket)
    qpos = P + np.arange(Ld)
    bias_past = _bias_tiles(rel_bias_table, _rel_bucket_np(np.arange(P)[None, :] - qpos[:, None])[None], None)
    bias_new = _bias_tiles(rel_bias_table, _rel_bucket_np(qpos[None, :] - qpos[:, None])[None], None)
    fg = final_g.reshape(1, D_MODEL)

    hp, hs = x_prompt, x_sample
    st_p, st_s = [], []
    for l in range(depth):
        p = _layer_params(l, norm1_g[l], w_in[l], da_lambda[l], da_subln_g[l], ml_conv_w[l], ml_conv_b[l],
                          ml_wq[l], ml_wk[l], ml_gate_b[l], ml_norm_g[l], ml_skip[l], cm_norm_g[l], cm_ws[l],
                          cm_b[l], w_out[l], norm2_g[l], peer_wq[l], peer_keys[l], peer_u[l], peer_v[l])
        last = l == depth - 1
        hp, sp = _layer(hp, None, p, bias_prompt, fg, last)
        past = (cache_k[l], cache_v[l], state_mlstm_c[l], state_mlstm_n[l], state_mlstm_m[l], state_mlstm_conv[l])
        hs, ss = _layer(hs, past, p, (bias_past, bias_new), fg, last)
        st_p.append(sp)
        st_s.append(ss)
    stack = lambda sts, i: jnp.stack([s[i] for s in sts])
    return (hp, hs,
            stack(st_p, 0), stack(st_p, 1), stack(st_p, 2), stack(st_p, 3), stack(st_p, 4), stack(st_p, 5),
            stack(st_s, 0), stack(st_s, 1), stack(st_s, 2), stack(st_s, 3), stack(st_s, 4), stack(st_s, 5),
            stack(st_s, 6))
```

```python
import functools
import math

import numpy as np
import jax
import jax.numpy as jnp
from jax import lax
from jax.experimental import pallas as pl
from jax.experimental.pallas import tpu as pltpu

F32 = jnp.float32
BF16 = jnp.bfloat16

D_MODEL = 1024
CHUNK = 64
DA_HEADS = 4
DA_HEAD_DIM = 64
DA_VDIM = 128
DA_WIDTH = 512
ML_HEADS = 4
ML_HEAD_DIM = 64
ML_WIDTH = 256
ML_CONV = 4
CM_GROUPS = 4
CM_WIDTH = 256
CM_GROUP_DIM = 64
CM_CHUNK = 128
PEER_HEADS = 8
PEER_KEYS = 128
PEER_EXPERTS = PEER_KEYS * PEER_KEYS
PEER_QDIM = 256
PEER_HALF = 128
PEER_TOPK = 16
REL_BUCKETS = 32
REL_MAX_DIST = 128
EPS = 1e-6
NEG_INF = -1e30

LANES = 128
SUBLANES = 8
MXU_N = 256
ATT_BLOCK = MXU_N
LOG2E = 1.4426950408889634
ML_SEQS_PER_STEP = 2
ML_PAD = ML_HEADS * LANES
VMEM_LIMIT = 48 * 1024 * 1024

_C_Q, _C_K, _C_V = 0, 512, 1024
_C_MC, _C_MV, _C_MO = 1536, 2048, 2560
_C_CU, _C_CV, _C_G = 3072, 3328, 3584
IN_PAD = 3712


def _cparams(sem):
    return pltpu.CompilerParams(dimension_semantics=sem, vmem_limit_bytes=VMEM_LIMIT)


def _gelu(x):
    return 0.5 * x * (1.0 + lax.erf(x * (2.0 ** -0.5)))


def _dot_nt(a, b):
    return lax.dot_general(a, b, (((1,), (1,)), ((), ())), preferred_element_type=F32)


def _dot_tn(a, b):
    return lax.dot_general(a, b, (((0,), (0,)), ((), ())), preferred_element_type=F32)


def _inproj_kernel(x_ref, g_ref, w_ref, q_ref, kf_ref, vf_ref, kb_ref, vb_ref,
                   mc_ref, mv_ref, mo_ref, cu_ref, cv_ref, gc_ref, *, blocked_t):
    x = x_ref[...]
    xn = x * lax.rsqrt(jnp.mean(x * x, axis=-1, keepdims=True) + EPS) * g_ref[...]
    xb = xn.astype(BF16)

    def proj(lo, hi):
        return jnp.dot(xb, w_ref[:, lo:hi], preferred_element_type=F32)

    def put(ref, val, scale=None):
        if scale is not None:
            val = val * scale
        if blocked_t:
            for n in range(ref.shape[0]):
                ref[n] = val[n * ATT_BLOCK:(n + 1) * ATT_BLOCK].T.astype(BF16)
        else:
            ref[...] = val.astype(BF16)

    put(q_ref, proj(_C_Q, _C_K), DA_HEAD_DIM ** -0.5 * (LOG2E if blocked_t else 1.0))
    k = proj(_C_K, _C_V)
    kf_ref[...] = k
    kb_ref[...] = k.astype(BF16)
    v = proj(_C_V, _C_MC)
    vf_ref[...] = v
    put(vb_ref, v)
    mc_ref[...] = proj(_C_MC, _C_MV)
    mv_ref[...] = proj(_C_MV, _C_MO)
    mo_ref[...] = proj(_C_MO, _C_CU)
    cu_ref[...] = proj(_C_CU, _C_CV)
    cv_ref[...] = proj(_C_CV, _C_G)
    gc_ref[...] = proj(_C_G, IN_PAD)


def _in_proj(x2d, g, w_pad, tm, blocked_t):
    T = x2d.shape[0]
    row = lambda n: pl.BlockSpec((tm, n), lambda i: (i, 0))
    full = lambda a: pl.BlockSpec(a.shape, lambda i: (0,) * a.ndim)
    if blocked_t:
        tspec = pl.BlockSpec((tm // ATT_BLOCK, DA_WIDTH, ATT_BLOCK), lambda i: (i, 0, 0))
        tshape = jax.ShapeDtypeStruct((T // ATT_BLOCK, DA_WIDTH, ATT_BLOCK), BF16)
    else:
        tspec, tshape = row(DA_WIDTH), jax.ShapeDtypeStruct((T, DA_WIDTH), BF16)
    outs = [(DA_WIDTH, F32), (DA_WIDTH, F32), (DA_WIDTH, BF16), None,
            (ML_PAD, F32), (ML_PAD, F32), (ML_PAD, F32), (CM_WIDTH, F32), (CM_WIDTH, F32), (LANES, F32)]
    specs = [tspec] + [tspec if o is None else row(o[0]) for o in outs]
    shapes = [tshape] + [tshape if o is None else jax.ShapeDtypeStruct((T, o[0]), o[1]) for o in outs]
    return pl.pallas_call(
        functools.partial(_inproj_kernel, blocked_t=blocked_t),
        grid=(T // tm,),
        in_specs=[row(D_MODEL), full(g), full(w_pad)],
        out_specs=specs,
        out_shape=shapes,
        compiler_params=_cparams(("parallel",)),
        name="in_proj",
    )(x2d, g, w_pad)


def _rel_bucket_np(rel):
    half = REL_BUCKETS // 2
    max_exact = half // 2
    ret = np.where(rel > 0, half, 0)
    n = np.abs(rel)
    nf = np.maximum(n, 1).astype(np.float32)
    large = max_exact + (np.log(nf / np.float32(max_exact)) / np.float32(math.log(REL_MAX_DIST / max_exact))
                         * np.float32(half - max_exact)).astype(np.int32)
    large = np.minimum(large, half - 1)
    return (ret + np.where(n < max_exact, n, large)).astype(np.int32)


def _bias_kernel(tab_ref, idx_ref, o_ref, *, shift_bucket, scale):
    for n in range(idx_ref.shape[0]):
        idx = idx_ref[n]
        for h in range(DA_HEADS):
            acc = jnp.zeros(idx.shape, F32)
            for b in range(REL_BUCKETS):
                acc = jnp.where(idx == b, tab_ref[b, h], acc)
            if shift_bucket is not None:
                acc = acc - tab_ref[shift_bucket, h]
            o_ref[n, h] = acc * scale


def _bias_tiles(table, idx_np, shift_bucket, scale=1.0):
    n, r, c = idx_np.shape
    return pl.pallas_call(
        functools.partial(_bias_kernel, shift_bucket=shift_bucket, scale=scale),
        in_specs=[pl.BlockSpec(memory_space=pltpu.SMEM), pl.BlockSpec(memory_space=pltpu.VMEM)],
        out_specs=pl.BlockSpec(memory_space=pltpu.VMEM),
        out_shape=jax.ShapeDtypeStruct((n, DA_HEADS, r, c), F32),
        name="rel_bias",
    )(table, jnp.asarray(idx_np))


def _lambda(lam_ref, lam_init):
    lp = lam_ref[...]
    a = jnp.sum(lp[0:1] * lp[1:2], axis=-1, keepdims=True)
    b = jnp.sum(lp[2:3] * lp[3:4], axis=-1, keepdims=True)
    return jnp.exp(a) - jnp.exp(b) + lam_init


def _subln(o1, l1, o2, l2, lam, g, lam_init):
    o = o1 / l1 - lam * (o2 / l2)
    o = o * lax.rsqrt(jnp.mean(o * o, axis=-1, keepdims=True) + EPS) * g
    return o * (1.0 - lam_init)


def _attn_prompt_kernel(lam_ref, qt_ref, k_ref, vt_ref, bias_ref, g_ref, o_ref,
                        m_sc, l_sc, a_sc, acc_sc, s_sc, p_sc, *, lam_init):
    tb = ATT_BLOCK
    qb = pl.program_id(1)
    lam = _lambda(lam_ref, lam_init)
    krow = lax.broadcasted_iota(jnp.int32, (tb, tb), 0)
    qcol = lax.broadcasted_iota(jnp.int32, (tb, tb), 1)
    diag_mask = (krow // CHUNK) <= (qcol // CHUNK)
    comp0 = lax.broadcasted_iota(jnp.int32, (LANES, 1), 0) < DA_HEAD_DIM
    has_prev = qb >= 1

    m_sc[...] = jnp.full(m_sc.shape, NEG_INF, F32)
    l_sc[...] = jnp.zeros(l_sc.shape, F32)
    acc_sc[...] = jnp.zeros(acc_sc.shape, F32)

    qts = []
    for h in range(DA_HEADS):
        qt = qt_ref[0, LANES * h:LANES * (h + 1), :]
        qts.append((jnp.where(comp0, qt, jnp.zeros_like(qt)), jnp.where(comp0, jnp.zeros_like(qt), qt)))

    def sweep(kblk, bias_n=None, mask=None):
        for h in range(DA_HEADS):
            kh = k_ref[0, pl.ds(pl.multiple_of(kblk * tb, tb), tb), LANES * h:LANES * (h + 1)]
            for c in range(2):
                s = jnp.dot(kh, qts[h][c], preferred_element_type=F32)
                if bias_n is not None:
                    s = s + bias_ref[bias_n, h]
                if mask is not None:
                    s = jnp.where(mask, s, NEG_INF)
                s_sc[2 * h + c] = s
        for ch in range(2 * DA_HEADS):
            s = s_sc[ch]
            m_old = m_sc[ch]
            m_new = jnp.maximum(m_old, jnp.max(s, axis=0, keepdims=True))
            alpha = jnp.exp2(m_old - m_new)
            p = jnp.exp2(s - m_new[0:1])
            l_sc[ch] = alpha * l_sc[ch] + jnp.sum(p, axis=0, keepdims=True)
            m_sc[ch] = m_new
            a_sc[ch] = alpha
            p_sc[ch] = p.astype(BF16)
        for h in range(DA_HEADS):
            vth = vt_ref[kblk, LANES * h:LANES * (h + 1), :]
            for c in range(2):
                ch = 2 * h + c
                acc_sc[ch] = a_sc[ch][0:1] * acc_sc[ch] + jnp.dot(vth, p_sc[ch], preferred_element_type=F32)

    sweep(qb, 0, diag_mask)
    sweep(jnp.maximum(qb - 1, 0), 1, has_prev)
    lax.fori_loop(0, jnp.maximum(qb - 1, 0), lambda j, _: (sweep(j), 0)[1], 0)

    for h in range(DA_HEADS):
        o = acc_sc[2 * h] / l_sc[2 * h][0:1] - lam * (acc_sc[2 * h + 1] / l_sc[2 * h + 1][0:1])
        o = o * lax.rsqrt(jnp.mean(o * o, axis=0, keepdims=True) + EPS) * g_ref[...]
        o_ref[0, :, LANES * h:LANES * (h + 1)] = (o * (1.0 - lam_init)).T.astype(o_ref.dtype)


def _attn_prompt(qt, k, vt, bias, lam_p, g_t, lam_init, B):
    S = k.shape[1]
    tb = ATT_BLOCK
    nb = S // tb
    return pl.pallas_call(
        functools.partial(_attn_prompt_kernel, lam_init=lam_init),
        grid=(B, nb),
        in_specs=[
            pl.BlockSpec(lam_p.shape, lambda b, i: (0, 0)),
            pl.BlockSpec((1, DA_WIDTH, tb), lambda b, i: (b * nb + i, 0, 0)),
            pl.BlockSpec((1, S, DA_WIDTH), lambda b, i: (b, 0, 0)),
            pl.BlockSpec((nb, DA_WIDTH, tb), lambda b, i: (b, 0, 0)),
            pl.BlockSpec(bias.shape, lambda b, i: (0, 0, 0, 0)),
            pl.BlockSpec(g_t.shape, lambda b, i: (0, 0)),
        ],
        out_specs=pl.BlockSpec((1, tb, DA_WIDTH), lambda b, i: (b, i, 0)),
        out_shape=jax.ShapeDtypeStruct((B, S, DA_WIDTH), BF16),
        scratch_shapes=[pltpu.VMEM((2 * DA_HEADS, SUBLANES, tb), F32),
                        pltpu.VMEM((2 * DA_HEADS, SUBLANES, tb), F32),
                        pltpu.VMEM((2 * DA_HEADS, SUBLANES, tb), F32),
                        pltpu.VMEM((2 * DA_HEADS, LANES, tb), F32),
                        pltpu.VMEM((2 * DA_HEADS, tb, tb), F32),
                        pltpu.VMEM((2 * DA_HEADS, tb, tb), BF16)],
        compiler_params=_cparams(("parallel", "arbitrary")),
        name="attn_prompt",
    )(lam_p, qt, k, vt, bias, g_t)


def _attn_step_kernel(lam_ref, q_ref, kn_ref, vn_ref, kp_ref, vp_ref, bp_ref, bn_ref, g_ref, o_ref, *, lam_init):
    lam = _lambda(lam_ref, lam_init)
    first = lax.broadcasted_iota(jnp.int32, (1, LANES), 1) < DA_HEAD_DIM
    g = g_ref[...]
    for h in range(DA_HEADS):
        sl = slice(LANES * h, LANES * (h + 1))
        q = q_ref[0, :, sl]
        kn = kn_ref[0, :, sl]
        vn = vn_ref[0, :, sl]
        kp = kp_ref[0, :, sl].astype(BF16)
        vp = vp_ref[0, :, sl].astype(BF16)
        res = []
        for c in range(2):
            qc = jnp.where(first if c == 0 else jnp.logical_not(first), q, jnp.zeros_like(q))
            sp = _dot_nt(qc, kp) + bp_ref[0, h]
            sn = _dot_nt(qc, kn) + bn_ref[0, h]
            m = jnp.maximum(jnp.max(sp, axis=-1, keepdims=True), jnp.max(sn, axis=-1, keepdims=True))
            pp = jnp.exp(sp - m)
            pn = jnp.exp(sn - m)
            l = jnp.sum(pp, axis=-1, keepdims=True) + jnp.sum(pn, axis=-1, keepdims=True)
            acc = (jnp.dot(pp.astype(BF16), vp, preferred_element_type=F32)
                   + jnp.dot(pn.astype(BF16), vn, preferred_element_type=F32))
            res.append((l, acc))
        (l1, a1), (l2, a2) = res
        o_ref[0, :, sl] = _subln(a1, l1, a2, l2, lam, g, lam_init).astype(o_ref.dtype)


def _attn_step(q, kn, vn, kp, vp, bias_p, bias_n, lam_p, g, lam_init):
    B, L, _ = q.shape
    P = kp.shape[1]
    c3 = lambda b: (b, 0, 0)
    z4 = lambda b: (0, 0, 0, 0)
    return pl.pallas_call(
        functools.partial(_attn_step_kernel, lam_init=lam_init),
        grid=(B,),
        in_specs=[
            pl.BlockSpec(lam_p.shape, lambda b: (0, 0)),
            pl.BlockSpec((1, L, DA_WIDTH), c3), pl.BlockSpec((1, L, DA_WIDTH), c3), pl.BlockSpec((1, L, DA_WIDTH), c3),
            pl.BlockSpec((1, P, DA_WIDTH), c3), pl.BlockSpec((1, P, DA_WIDTH), c3),
            pl.BlockSpec(bias_p.shape, z4), pl.BlockSpec(bias_n.shape, z4),
            pl.BlockSpec(g.shape, lambda b: (0, 0)),
        ],
        out_specs=pl.BlockSpec((1, L, DA_WIDTH), c3),
        out_shape=jax.ShapeDtypeStruct((B, L, DA_WIDTH), BF16),
        compiler_params=_cparams(("parallel",)),
        name="attn_step",
    )(lam_p, q, kn, vn, kp, vp, bias_p, bias_n, g)


def _log_sigmoid(x):
    return jnp.minimum(x, 0.0) - jnp.log(1.0 + jnp.exp(-jnp.abs(x)))


def _mlstm_kernel(gb_ref, mc_ref, mv_ref, mo_ref, gc_ref, cw_ref, cb_ref, wq_ref, wk_ref, ng_ref, sk_ref,
                  c0_ref, n0_ref, m0_ref, cv0_ref,
                  o_ref, co_ref, no_ref, mo2_ref, cvo_ref,
                  xbuf, c_sc, n_sc, m_sc):
    j = pl.program_id(1)
    L = CHUNK
    bb = mc_ref.shape[0]
    chains = [(b, h) for b in range(bb) for h in range(ML_HEADS)]
    hsl = lambda h: slice(LANES * h, LANES * (h + 1))

    @pl.when(j == 0)
    def _():
        xbuf[:, 5:8, :] = cv0_ref[...]
        c_sc[...] = c0_ref[...]
        n_sc[...] = n0_ref[...]
        m_sc[...] = m0_ref[...]

    r = lax.broadcasted_iota(jnp.int32, (L, L), 0)
    s = lax.broadcasted_iota(jnp.int32, (L, L), 1)
    tril = s <= r

    cc, gcol, gt = [], [], []
    for b in range(bb):
        x = mc_ref[b]
        xbuf[b, 8:8 + L, :] = x
        y = (cb_ref[...] + cw_ref[3:4, :] * x + cw_ref[2:3, :] * xbuf[b, 7:7 + L, :]
             + cw_ref[1:2, :] * xbuf[b, 6:6 + L, :] + cw_ref[0:1, :] * xbuf[b, 5:5 + L, :])
        cc.append(y * jax.nn.sigmoid(y))
        tail = xbuf[b, 5 + L:8 + L, :]
        xbuf[b, 5:8, :] = tail
        cvo_ref[b] = tail
        g = gc_ref[b]
        gcol.append(g)
        gt.append(jnp.concatenate([g, jnp.zeros((LANES - L, LANES), F32)], axis=0).T)

    qs, ks = {}, {}
    for b, h in chains:
        cch = cc[b][:, hsl(h)]
        qs[b, h] = jnp.dot(cch, wq_ref[h], preferred_element_type=F32)
        ks[b, h] = jnp.dot(cch, wk_ref[h], preferred_element_type=F32) * (ML_HEAD_DIM ** -0.5)
    sc, rd = {}, {}
    for b, h in chains:
        sc[b, h] = _dot_nt(qs[b, h], ks[b, h])
        rd[b, h] = jnp.dot(qs[b, h], c_sc[b, h], preferred_element_type=F32)
    ig_c, ig_r, lf_c, lf_r = {}, {}, {}, {}
    for b, h in chains:
        b_i = gb_ref[0, h]
        b_f = gb_ref[1, h]
        ig_c[b, h] = gcol[b][:, h:h + 1] + b_i
        lf_c[b, h] = _log_sigmoid(gcol[b][:, 4 + h:5 + h] + b_f)
        ig_r[b, h] = gt[b][h:h + 1, 0:L] + b_i
        lf_r[b, h] = _log_sigmoid(gt[b][4 + h:5 + h, 0:L] + b_f)
    f_c, f_r = {}, {}
    for ch in chains:
        f_c[ch] = jnp.sum(jnp.where(tril, lf_r[ch], 0.0), axis=1, keepdims=True)
        f_r[ch] = jnp.sum(jnp.where(r <= s, lf_c[ch], 0.0), axis=0, keepdims=True)
    dmat, dmax, tmax, m_prev, fl = {}, {}, {}, {}, {}
    for b, h in chains:
        ch = (b, h)
        m_prev[ch] = m_sc[b, h][0:1, 0:1]
        fl[ch] = f_c[ch][L - 1:L, :]
        dmat[ch] = jnp.where(tril, f_c[ch] - f_r[ch] + ig_r[ch], NEG_INF)
        dmax[ch] = jnp.max(dmat[ch], axis=1, keepdims=True)
        tmax[ch] = jnp.max(fl[ch] - f_r[ch] + ig_r[ch], axis=1, keepdims=True)
    sw, iw, m_t, kw, wc, m_new = {}, {}, {}, {}, {}, {}
    for ch in chains:
        inter = f_c[ch] + m_prev[ch]
        m_t[ch] = jnp.maximum(inter, dmax[ch])
        sw[ch] = sc[ch] * jnp.exp(dmat[ch] - m_t[ch])
        iw[ch] = jnp.exp(inter - m_t[ch])
        m_new[ch] = jnp.maximum(fl[ch] + m_prev[ch], tmax[ch])
        wc[ch] = jnp.exp(fl[ch] + m_prev[ch] - m_new[ch])
        kw[ch] = ks[ch] * jnp.exp(fl[ch] - f_c[ch] + ig_c[ch] - m_new[ch])
    nv, up = {}, {}
    for b, h in chains:
        vh = mv_ref[b, :, hsl(h)]
        nv[b, h] = jnp.dot(sw[b, h], vh, preferred_element_type=F32)
        up[b, h] = _dot_tn(kw[b, h], vh)
    n_prev, den, ksum = {}, {}, {}
    for b, h in chains:
        ch = (b, h)
        n_prev[ch] = n_sc[b, h]
        den[ch] = (jnp.sum(sw[ch], axis=1, keepdims=True)
                   + iw[ch] * jnp.sum(qs[ch] * n_prev[ch], axis=1, keepdims=True))
        ksum[ch] = jnp.sum(kw[ch], axis=0, keepdims=True)
    hout, ssq = {}, {}
    for ch in chains:
        hout[ch] = (nv[ch] + iw[ch] * rd[ch]) / jnp.maximum(jnp.abs(den[ch]), jnp.exp(-m_t[ch]))
        ssq[ch] = jnp.sum(hout[ch] * hout[ch], axis=1, keepdims=True)
    for b, h in chains:
        ch = (b, h)
        c_new = wc[ch] * c_sc[b, h] + up[ch]
        n_new = wc[ch] * n_prev[ch] + ksum[ch]
        m_row = jnp.broadcast_to(m_new[ch], (1, LANES))
        c_sc[b, h] = c_new
        n_sc[b, h] = n_new
        m_sc[b, h] = m_row
        co_ref[b, h] = c_new
        no_ref[b, h] = n_new
        mo2_ref[b, h] = m_row
        hn = hout[ch] * lax.rsqrt(ssq[ch] * (1.0 / ML_HEAD_DIM) + EPS) * ng_ref[:, hsl(h)]
        o_ref[b, :, hsl(h)] = ((hn + sk_ref[:, hsl(h)] * cc[b][:, hsl(h)])
                               * jax.nn.sigmoid(mo_ref[b, :, hsl(h)])).astype(o_ref.dtype)


def _mlstm(gate_b, mc, mv, mo, gc, cw, cb, wq, wk, ng, sk, c0, n0, m0, cv0, bb):
    B, L, _ = mc.shape
    nc = L // CHUNK
    tok = lambda n: pl.BlockSpec((bb, CHUNK, n), lambda b, j: (b, j, 0))
    full = lambda a: pl.BlockSpec(a.shape, lambda b, j: (0,) * a.ndim)
    st4 = lambda a: pl.BlockSpec((bb,) + a.shape[1:], lambda b, j: (b,) + (0,) * (a.ndim - 1))
    return pl.pallas_call(
        _mlstm_kernel,
        grid=(B // bb, nc),
        in_specs=[pl.BlockSpec(memory_space=pltpu.SMEM),
                  tok(ML_PAD), tok(ML_PAD), tok(ML_PAD), tok(LANES),
                  full(cw), full(cb), full(wq), full(wk), full(ng), full(sk),
                  st4(c0), st4(n0), st4(m0), st4(cv0)],
        out_specs=[tok(ML_PAD), st4(c0), st4(n0), st4(m0), st4(cv0)],
        out_shape=[jax.ShapeDtypeStruct((B, L, ML_PAD), BF16),
                   jax.ShapeDtypeStruct(c0.shape, F32), jax.ShapeDtypeStruct(n0.shape, F32),
                   jax.ShapeDtypeStruct(m0.shape, F32), jax.ShapeDtypeStruct(cv0.shape, F32)],
        scratch_shapes=[pltpu.VMEM((bb, 8 + CHUNK, ML_PAD), F32),
                        pltpu.VMEM((bb, ML_HEADS, LANES, LANES), F32),
                        pltpu.VMEM((bb, ML_HEADS, 1, LANES), F32),
                        pltpu.VMEM((bb, ML_HEADS, 1, LANES), F32)],
        compiler_params=_cparams(("parallel", "arbitrary")),
        name="mlstm",
    )(gate_b, mc, mv, mo, gc, cw, cb, wq, wk, ng, sk, c0, n0, m0, cv0)


def _mixout_kernel(x_ref, oda_ref, oml_ref, cu_ref, cv_ref, cmg_ref, ws_ref, bm_ref,
                   wda_ref, wml_ref, wcm_ref, x1_ref, vcm_ref, *, lc):
    tm = x_ref.shape[0]
    u = _gelu(cu_ref[...])
    gv = _gelu(cv_ref[...])
    vcm = gv * lax.rsqrt(jnp.mean(gv * gv, axis=-1, keepdims=True) + EPS) * cmg_ref[...]
    vcm_ref[...] = vcm
    r = lax.broadcasted_iota(jnp.int32, (lc, lc), 0)
    c = lax.broadcasted_iota(jnp.int32, (lc, lc), 1)
    grp = lax.broadcasted_iota(jnp.int32, (1, CM_WIDTH), 1) // CM_GROUP_DIM
    wsm = [jnp.where(c <= r, ws_ref[g, 0:lc, 0:lc], 0.0).astype(BF16) for g in range(CM_GROUPS)]
    pieces = []
    for ci in range(tm // lc):
        vch = vcm[ci * lc:(ci + 1) * lc].astype(BF16)
        mixed = bm_ref[0:lc, :]
        for g in range(CM_GROUPS):
            mixed = mixed + jnp.where(grp == g, jnp.dot(wsm[g], vch, preferred_element_type=F32), 0.0)
        pieces.append(u[ci * lc:(ci + 1) * lc] * mixed)
    ocm = jnp.concatenate(pieces, axis=0) if len(pieces) > 1 else pieces[0]
    y = (jnp.dot(oda_ref[...], wda_ref[...], preferred_element_type=F32)
         + jnp.dot(oml_ref[...], wml_ref[...], preferred_element_type=F32)
         + jnp.dot(ocm.astype(BF16), wcm_ref[...], preferred_element_type=F32))
    x1_ref[...] = x_ref[...] + y


def _mix_out(x2d, oda, oml, cu, cv, cmg, ws, bmat, wda, wml, wcm, tm, lc):
    T = x2d.shape[0]
    row = lambda n: pl.BlockSpec((tm, n), lambda i: (i, 0))
    full = lambda a: pl.BlockSpec(a.shape, lambda i: (0,) * a.ndim)
    return pl.pallas_call(
        functools.partial(_mixout_kernel, lc=lc),
        grid=(T // tm,),
        in_specs=[row(D_MODEL), row(DA_WIDTH), row(ML_PAD), row(CM_WIDTH), row(CM_WIDTH),
                  full(cmg), full(ws), full(bmat), full(wda), full(wml), full(wcm)],
        out_specs=[row(D_MODEL), row(CM_WIDTH)],
        out_shape=[jax.ShapeDtypeStruct((T, D_MODEL), F32), jax.ShapeDtypeStruct((T, CM_WIDTH), F32)],
        compiler_params=_cparams(("parallel",)),
        name="mix_out",
    )(x2d, oda, oml, cu, cv, cmg, ws, bmat, wda, wml, wcm)


def _peer_query_kernel(x_ref, g_ref, wq_ref, keys_ref, xnt_ref, st_ref):
    x = x_ref[...]
    xf = x * lax.rsqrt(jnp.mean(x * x, axis=-1, keepdims=True) + EPS) * g_ref[...]
    xn = xf.astype(BF16)
    xnt_ref[...] = xf.T.astype(BF16)
    for h in range(PEER_HEADS):
        q = jnp.dot(xn, wq_ref[:, PEER_QDIM * h:PEER_QDIM * (h + 1)], preferred_element_type=F32)
        q = q * lax.rsqrt(jnp.mean(q * q, axis=-1, keepdims=True) + EPS)
        for c in range(2):
            qc = q[:, PEER_HALF * c:PEER_HALF * (c + 1)].astype(BF16)
            st_ref[h, c] = _dot_nt(keys_ref[h, c], qc)


def _peer_query(x1, g, wq, keys, tm):
    T = x1.shape[0]
    full = lambda a: pl.BlockSpec(a.shape, lambda i: (0,) * a.ndim)
    return pl.pallas_call(
        _peer_query_kernel,
        grid=(T // tm,),
        in_specs=[pl.BlockSpec((tm, D_MODEL), lambda i: (i, 0)), full(g), full(wq), full(keys)],
        out_specs=[pl.BlockSpec((D_MODEL, tm), lambda i: (0, i)),
                   pl.BlockSpec((PEER_HEADS, 2, PEER_KEYS, tm), lambda i: (0, 0, 0, i))],
        out_shape=[jax.ShapeDtypeStruct((D_MODEL, T), BF16),
                   jax.ShapeDtypeStruct((PEER_HEADS, 2, PEER_KEYS, T), F32)],
        compiler_params=_cparams(("parallel",)),
        name="peer_query",
    )(x1, g, wq, keys)


_PAIRS = [(a, b) for a in range(PEER_TOPK) for b in range(PEER_TOPK) if (a + 1) * (b + 1) <= PEER_TOPK]
_NCAND = ((len(_PAIRS) + 7) // 8) * 8


def _peer_select_kernel(st_ref, s1_ref, s2_ref, thr_ref, vals, cand):
    tt = st_ref.shape[-1]
    for h in range(PEER_HEADS):
        for c in range(2):
            cur = st_ref[h, c]
            for i in range(PEER_TOPK):
                m = jnp.max(cur, axis=0, keepdims=True)
                vals[c, i:i + 1, :] = m
                cur = jnp.where(cur == m, NEG_INF, cur)
        cand[...] = jnp.full((_NCAND, tt), NEG_INF, F32)
        for i, (a, b) in enumerate(_PAIRS):
            cand[i:i + 1, :] = vals[0, a:a + 1, :] + vals[1, b:b + 1, :]
        top1 = vals[0, 0:1, :]
        top2 = vals[1, 0:1, :]
        best = top1 + top2
        cur = cand[...]
        z = jnp.zeros((1, tt), F32)
        thr = best
        for i in range(PEER_TOPK):
            thr = jnp.max(cur, axis=0, keepdims=True)
            z = z + jnp.exp(thr - best)
            cur = jnp.where(cur == thr, NEG_INF, cur)
        shift = best + jnp.log(2.0 * z)
        s1_ref[h] = (st_ref[h, 0] - shift) * LOG2E
        s2_ref[h] = st_ref[h, 1] * LOG2E
        thr_ref[h:h + 1, :] = (thr - shift) * LOG2E


def _peer_select(st, tt):
    T = st.shape[-1]
    spec = pl.BlockSpec((PEER_HEADS, PEER_KEYS, tt), lambda i: (0, 0, i))
    shp = jax.ShapeDtypeStruct((PEER_HEADS, PEER_KEYS, T), F32)
    return pl.pallas_call(
        _peer_select_kernel,
        grid=(T // tt,),
        in_specs=[pl.BlockSpec((PEER_HEADS, 2, PEER_KEYS, tt), lambda i: (0, 0, 0, i))],
        out_specs=[spec, spec, pl.BlockSpec((PEER_HEADS, tt), lambda i: (0, i))],
        out_shape=[shp, shp, jax.ShapeDtypeStruct((PEER_HEADS, T), F32)],
        scratch_shapes=[pltpu.VMEM((2, PEER_TOPK, tt), F32), pltpu.VMEM((_NCAND, tt), F32)],
        compiler_params=_cparams(("parallel",)),
        name="peer_select",
    )(st)


def _peer_dense_kernel(x1_ref, xnt_ref, s1_ref, s2_ref, thr_ref, u_ref, vt_ref, fg_ref, o_ref,
                       acc_ref, ht_ref, at_ref, *, rows, final_norm):
    j = pl.program_id(1)
    last = pl.num_programs(1) - 1

    @pl.when(j == 0)
    def _():
        acc_ref[...] = jnp.zeros_like(acc_ref)

    per = MXU_N // PEER_KEYS
    nchunk = rows // per
    tt = ht_ref.shape[1]
    lw = LANES
    assert rows == SUBLANES
    i0 = pl.multiple_of(j * rows, SUBLANES)

    def proj(c):
        cs = slice(c * MXU_N, (c + 1) * MXU_N)
        ht_ref[cs, :] = jnp.dot(u_ref[cs, :], xnt_ref[...], preferred_element_type=F32)

    def gate(c):
        rr = range(c * per, (c + 1) * per)
        for lt in range(tt // lw):
            ls = slice(lt * lw, (lt + 1) * lw)
            thr = [jnp.broadcast_to(thr_ref[h:h + 1, ls], (SUBLANES, lw)) for h in range(PEER_HEADS)]
            s1t = [s1_ref[h, pl.ds(i0, rows), ls] for h in range(PEER_HEADS)]
            s1v = [[jnp.broadcast_to(s1t[h][r:r + 1], (SUBLANES, lw)) for h in range(PEER_HEADS)] for r in rr]
            for sg in range(PEER_KEYS // (2 * SUBLANES)):
                halves = [[] for _ in rr]
                for half in range(2):
                    k0 = sg * 2 * SUBLANES + half * SUBLANES
                    ks = slice(k0, k0 + SUBLANES)
                    s2v = [s2_ref[h, ks, ls] for h in range(PEER_HEADS)]
                    for ri, r in enumerate(rr):
                        t = []
                        for h in range(PEER_HEADS):
                            lg = s2v[h] + s1v[ri][h]
                            t.append(jnp.where(lg >= thr[h], jnp.exp2(lg), 0.0))
                        w = ((t[0] + t[1]) + (t[2] + t[3])) + ((t[4] + t[5]) + (t[6] + t[7]))
                        ht = ht_ref[r * PEER_KEYS + k0:r * PEER_KEYS + k0 + SUBLANES, ls]
                        halves[ri].append(w * (ht * (1.0 + lax.erf(ht * (2.0 ** -0.5)))))
                for ri in range(per):
                    o0 = ri * PEER_KEYS + sg * 2 * SUBLANES
                    at_ref[c, o0:o0 + 2 * SUBLANES, ls] = jnp.concatenate(halves[ri], axis=0).astype(BF16)

    def value(c):
        acc_ref[...] += jnp.dot(vt_ref[:, c * MXU_N:(c + 1) * MXU_N], at_ref[c], preferred_element_type=F32)

    proj(0)
    if nchunk > 1:
        proj(1)
    for c in range(nchunk):
        gate(c)
        value(c)
        if c + 2 < nchunk:
            proj(c + 2)

    @pl.when(j == last)
    def _():
        x2 = x1_ref[...] + acc_ref[...].T
        if final_norm:
            x2 = x2 * lax.rsqrt(jnp.mean(x2 * x2, axis=-1, keepdims=True) + EPS) * fg_ref[...]
        o_ref[...] = x2


def _peer_dense(x1, xn, s1, s2, thr, u_bf, vt_bf, fg, tt, eb, final_norm):
    T = x1.shape[0]
    rows = eb // PEER_KEYS
    ne = PEER_EXPERTS // eb
    hk = pl.BlockSpec((PEER_HEADS, PEER_KEYS, tt), lambda i, j: (0, 0, i))
    return pl.pallas_call(
        functools.partial(_peer_dense_kernel, rows=rows, final_norm=final_norm),
        grid=(T // tt, ne),
        in_specs=[pl.BlockSpec((tt, D_MODEL), lambda i, j: (i, 0)),
                  pl.BlockSpec((D_MODEL, tt), lambda i, j: (0, i)),
                  hk, hk, pl.BlockSpec((PEER_HEADS, tt), lambda i, j: (0, i)),
                  pl.BlockSpec((eb, D_MODEL), lambda i, j: (j, 0)),
                  pl.BlockSpec((D_MODEL, eb), lambda i, j: (0, j)),
                  pl.BlockSpec(fg.shape, lambda i, j: (0, 0))],
        out_specs=pl.BlockSpec((tt, D_MODEL), lambda i, j: (i, 0)),
        out_shape=jax.ShapeDtypeStruct((T, D_MODEL), F32),
        scratch_shapes=[pltpu.VMEM((D_MODEL, tt), F32), pltpu.VMEM((eb, tt), F32),
                        pltpu.VMEM((eb // MXU_N, MXU_N, tt), BF16)],
        compiler_params=_cparams(("parallel", "arbitrary")),
        name="peer_dense",
    )(x1, xn, s1, s2, thr, u_bf, vt_bf, fg)


def _pad_heads(w, axis=-1):
    w = jnp.moveaxis(w, axis, -1)
    lead = w.shape[:-1]
    w = w.reshape(lead + (ML_HEADS, ML_HEAD_DIM))
    w = jnp.pad(w, [(0, 0)] * len(lead) + [(0, 0), (0, LANES - ML_HEAD_DIM)])
    return jnp.moveaxis(w.reshape(lead + (ML_PAD,)), -1, axis)


def _layer_params(l, norm1_g, w_in, da_lambda, da_subln_g, ml_conv_w, ml_conv_b, ml_wq, ml_wk, ml_gate_b,
                  ml_norm_g, ml_skip, cm_norm_g, cm_ws, cm_b, w_out, norm2_g, peer_wq, peer_keys, peer_u, peer_v):
    o = np.cumsum((0, 512, 512, 512, 256, 256, 256, 4, 4, 256, 256))
    seg = lambda i: w_in[:, o[i]:o[i + 1]]
    gates = jnp.pad(jnp.concatenate([seg(6), seg(7)], axis=1), ((0, 0), (0, LANES - 2 * ML_HEADS)))
    w_pad = jnp.concatenate([seg(0), seg(1), seg(2), _pad_heads(seg(3)), _pad_heads(seg(4)), _pad_heads(seg(5)),
                             seg(8), seg(9), gates], axis=1).astype(BF16)
    pad_sq = lambda w: jnp.pad(w, ((0, 0), (0, LANES - ML_HEAD_DIM), (0, LANES - ML_HEAD_DIM)))
    return dict(
        lam_init=0.8 - 0.6 * math.exp(-0.3 * l),
        g1=norm1_g.reshape(1, D_MODEL), w_pad=w_pad,
        lam_p=da_lambda, subln_g=da_subln_g.reshape(1, DA_VDIM),
        cw=_pad_heads(ml_conv_w), cb=_pad_heads(ml_conv_b.reshape(1, ML_WIDTH)),
        wq=pad_sq(ml_wq), wk=pad_sq(ml_wk), gate_b=ml_gate_b,
        ng=_pad_heads(ml_norm_g.reshape(1, ML_WIDTH)), sk=_pad_heads(ml_skip.reshape(1, ML_WIDTH)),
        cmg=cm_norm_g.reshape(1, CM_WIDTH), ws=cm_ws,
        bmat=jnp.repeat(cm_b.T, CM_GROUP_DIM, axis=1),
        wda=w_out[:DA_WIDTH].astype(BF16),
        wml=_pad_heads(w_out[DA_WIDTH:DA_WIDTH + ML_WIDTH], axis=0).astype(BF16),
        wcm=w_out[DA_WIDTH + ML_WIDTH:].astype(BF16),
        g2=norm2_g.reshape(1, D_MODEL), pwq=peer_wq.astype(BF16), pkeys=peer_keys.astype(BF16),
        u_bf=peer_u.astype(BF16), vt_bf=peer_v.T.astype(BF16),
    )


def _pick(n, prefs):
    for p in prefs:
        if n % p == 0:
            return p
    return n


def _layer(x, past, p, bias, final_g, final_norm):
    B, L, _ = x.shape
    T = B * L
    x2d = x.reshape(T, D_MODEL)
    tm = _pick(T, (512, 256, 128))
    q, kf, vf, kb, vb, mc, mv, mo, cu, cv, gc = _in_proj(x2d, p["g1"], p["w_pad"], tm, past is None)
    r3 = lambda a: a.reshape(B, L, a.shape[-1])
    if past is None:
        g_t = jnp.broadcast_to(p["subln_g"].reshape(DA_VDIM, 1), (DA_VDIM, ATT_BLOCK))
        oda = _attn_prompt(q, r3(kb), vb, bias, p["lam_p"], g_t, p["lam_init"], B)
        c0 = jnp.zeros((B, ML_HEADS, LANES, LANES), F32)
        n0 = jnp.zeros((B, ML_HEADS, 1, LANES), F32)
        m0 = jnp.zeros((B, ML_HEADS, 1, LANES), F32)
        cv0 = jnp.zeros((B, ML_CONV - 1, ML_PAD), F32)
        lc = CM_CHUNK
    else:
        pk, pv, pc, pn, pm, pconv = past
        P = pk.shape[1]
        oda = _attn_step(r3(q), r3(kb), r3(vb), pk.reshape(B, P, DA_WIDTH), pv.reshape(B, P, DA_WIDTH),
                         bias[0], bias[1], p["lam_p"], p["subln_g"], p["lam_init"])
        c0 = jnp.pad(pc, ((0, 0), (0, 0), (0, LANES - ML_HEAD_DIM), (0, LANES - ML_HEAD_DIM)))
        n0 = jnp.pad(pn, ((0, 0), (0, 0), (0, LANES - ML_HEAD_DIM)))[:, :, None, :]
        m0 = jnp.broadcast_to(pm[:, :, None, None], (B, ML_HEADS, 1, LANES))
        cv0 = _pad_heads(pconv)
        lc = L
    oml, c_new, n_new, m_new, conv_new = _mlstm(
        p["gate_b"], r3(mc), r3(mv), r3(mo), r3(gc), p["cw"], p["cb"], p["wq"], p["wk"], p["ng"], p["sk"],
        c0, n0, m0, cv0, ML_SEQS_PER_STEP)
    x1, vcm = _mix_out(x2d, oda.reshape(T, DA_WIDTH), oml.reshape(T, ML_PAD), cu, cv, p["cmg"], p["ws"], p["bmat"],
                       p["wda"], p["wml"], p["wcm"], _pick(T, (256, 128)), lc)
    xn2, st = _peer_query(x1, p["g2"], p["pwq"], p["pkeys"], _pick(T, (256, 128)))
    s1, s2, thr = _peer_select(st, _pick(T, (128,)))
    x2 = _peer_dense(x1, xn2, s1, s2, thr, p["u_bf"], p["vt_bf"], final_g, _pick(T, (512, 256, 128)), 1024, final_norm)
    state = (kf.reshape(B, L, DA_HEADS, 2 * DA_HEAD_DIM), vf.reshape(B, L, DA_HEADS, DA_VDIM),
             c_new[:, :, :ML_HEAD_DIM, :ML_HEAD_DIM], n_new[:, :, 0, :ML_HEAD_DIM], m_new[:, :, 0, 0],
             conv_new.reshape(B, ML_CONV - 1, ML_HEADS, LANES)[..., :ML_HEAD_DIM].reshape(B, ML_CONV - 1, ML_WIDTH),
             vcm.reshape(B, L, CM_WIDTH))
    return x2.reshape(B, L, D_MODEL), state


def kernel(x_prompt, x_sample, cache_k, cache_v, state_mlstm_c, state_mlstm_n, state_mlstm_m, state_mlstm_conv, norm1_g, w_in, da_lambda, da_subln_g, rel_bias_table, ml_conv_w, ml_conv_b, ml_wq, ml_wk, ml_gate_b, ml_norm_g, ml_skip, cm_norm_g, cm_ws, cm_b, w_out, norm2_g, peer_wq, peer_keys, peer_u, peer_v, final_g):
    depth = w_in.shape[0]
    S = x_prompt.shape[1]
    Ld = x_sample.shape[1]
    P = cache_k.shape[2]
    assert S % ATT_BLOCK == 0
    far_bucket = int(_rel_bucket_np(np.array(-(ATT_BLOCK + 1))))
    ii = np.arange(ATT_BLOCK)
    idx_prompt = np.stack([_rel_bucket_np(ii[:, None] - ii[None, :]),
                           _rel_bucket_np(ii[:, None] - ATT_BLOCK - ii[None, :])])
    bias_prompt = _bias_tiles(rel_bias_table, idx_prompt, far_bucket, LOG2E)
    qpos = P + np.arange(Ld)
    bias_past = _bias_tiles(rel_bias_table, _rel_bucket_np(np.arange(P)[None, :] - qpos[:, None])[None], None)
    bias_new = _bias_tiles(rel_bias_table, _rel_bucket_np(qpos[None, :] - qpos[:, None])[None], None)
    fg = final_g.reshape(1, D_MODEL)

    hp, hs = x_prompt, x_sample
    st_p, st_s = [], []
    for l in range(depth):
        p = _layer_params(l, norm1_g[l], w_in[l], da_lambda[l], da_subln_g[l], ml_conv_w[l], ml_conv_b[l],
                          ml_wq[l], ml_wk[l], ml_gate_b[l], ml_norm_g[l], ml_skip[l], cm_norm_g[l], cm_ws[l],
                          cm_b[l], w_out[l], norm2_g[l], peer_wq[l], peer_keys[l], peer_u[l], peer_v[l])
        last = l == depth - 1
        hp, sp = _layer(hp, None, p, bias_prompt, fg, last)
        past = (cache_k[l], cache_v[l], state_mlstm_c[l], state_mlstm_n[l], state_mlstm_m[l], state_mlstm_conv[l])
        hs, ss = _layer(hs, past, p, (bias_past, bias_new), fg, last)
        st_p.append(sp)
        st_s.append(ss)
    stack = lambda sts, i: jnp.stack([s[i] for s in sts])
    return (hp, hs,
            stack(st_p, 0), stack(st_p, 1), stack(st_p, 2), stack(st_p, 3), stack(st_p, 4), stack(st_p, 5),
            stack(st_s, 0), stack(st_s, 1), stack(st_s, 2), stack(st_s, 3), stack(st_s, 4), stack(st_s, 5),
            stack(st_s, 6))
```

```python
import functools
import math

import numpy as np
import jax
import jax.numpy as jnp
from jax import lax
from jax.experimental import pallas as pl
from jax.experimental.pallas import tpu as pltpu

F32 = jnp.float32
BF16 = jnp.bfloat16

D_MODEL = 1024
CHUNK = 64
DA_HEADS = 4
DA_HEAD_DIM = 64
DA_VDIM = 128
DA_WIDTH = 512
ML_HEADS = 4
ML_HEAD_DIM = 64
ML_WIDTH = 256
ML_CONV = 4
CM_GROUPS = 4
CM_WIDTH = 256
CM_GROUP_DIM = 64
CM_CHUNK = 128
PEER_HEADS = 8
PEER_KEYS = 128
PEER_EXPERTS = PEER_KEYS * PEER_KEYS
PEER_QDIM = 256
PEER_HALF = 128
PEER_TOPK = 16
REL_BUCKETS = 32
REL_MAX_DIST = 128
EPS = 1e-6
NEG_INF = -1e30

LANES = 128
SUBLANES = 8
MXU_N = 256
ATT_BLOCK = MXU_N
LOG2E = 1.4426950408889634
ML_SEQS_PER_STEP = 4
ML_PAD = ML_HEADS * LANES
VMEM_LIMIT = 56 * 1024 * 1024
PEER_BLOCK_EXPERTS = 2048

_C_Q, _C_K, _C_V = 0, 512, 1024
_C_MC, _C_MV, _C_MO = 1536, 2048, 2560
_C_CU, _C_CV, _C_G = 3072, 3328, 3584
IN_PAD = 3712


def _cparams(sem):
    return pltpu.CompilerParams(dimension_semantics=sem, vmem_limit_bytes=VMEM_LIMIT)


def _gelu(x):
    return 0.5 * x * (1.0 + lax.erf(x * (2.0 ** -0.5)))


def _dot_nt(a, b):
    return lax.dot_general(a, b, (((1,), (1,)), ((), ())), preferred_element_type=F32)


def _dot_tn(a, b):
    return lax.dot_general(a, b, (((0,), (0,)), ((), ())), preferred_element_type=F32)


def _inproj_kernel(x_ref, g_ref, w_ref, q_ref, kf_ref, vf_ref, kb_ref, vb_ref,
                   mc_ref, mv_ref, mo_ref, cu_ref, cv_ref, gc_ref, *, blocked_t):
    x = x_ref[...]
    xn = x * lax.rsqrt(jnp.mean(x * x, axis=-1, keepdims=True) + EPS) * g_ref[...]
    xb = xn.astype(BF16)

    def proj(lo, hi):
        return jnp.dot(xb, w_ref[:, lo:hi], preferred_element_type=F32)

    def put(ref, val, scale=None):
        if scale is not None:
            val = val * scale
        if blocked_t:
            for n in range(ref.shape[0]):
                ref[n] = val[n * ATT_BLOCK:(n + 1) * ATT_BLOCK].T.astype(BF16)
        else:
            ref[...] = val.astype(BF16)

    put(q_ref, proj(_C_Q, _C_K), DA_HEAD_DIM ** -0.5 * (LOG2E if blocked_t else 1.0))
    k = proj(_C_K, _C_V)
    kf_ref[...] = k
    kb_ref[...] = k.astype(BF16)
    v = proj(_C_V, _C_MC)
    vf_ref[...] = v
    put(vb_ref, v)
    mc_ref[...] = proj(_C_MC, _C_MV)
    mv_ref[...] = proj(_C_MV, _C_MO)
    mo_ref[...] = proj(_C_MO, _C_CU)
    cu_ref[...] = proj(_C_CU, _C_CV)
    cv_ref[...] = proj(_C_CV, _C_G)
    gc_ref[...] = proj(_C_G, IN_PAD)


def _in_proj(x2d, g, w_pad, tm, blocked_t):
    T = x2d.shape[0]
    row = lambda n: pl.BlockSpec((tm, n), lambda i: (i, 0))
    full = lambda a: pl.BlockSpec(a.shape, lambda i: (0,) * a.ndim)
    if blocked_t:
        tspec = pl.BlockSpec((tm // ATT_BLOCK, DA_WIDTH, ATT_BLOCK), lambda i: (i, 0, 0))
        tshape = jax.ShapeDtypeStruct((T // ATT_BLOCK, DA_WIDTH, ATT_BLOCK), BF16)
    else:
        tspec, tshape = row(DA_WIDTH), jax.ShapeDtypeStruct((T, DA_WIDTH), BF16)
    outs = [(DA_WIDTH, F32), (DA_WIDTH, F32), (DA_WIDTH, BF16), None,
            (ML_PAD, F32), (ML_PAD, F32), (ML_PAD, F32), (CM_WIDTH, F32), (CM_WIDTH, F32), (LANES, F32)]
    specs = [tspec] + [tspec if o is None else row(o[0]) for o in outs]
    shapes = [tshape] + [tshape if o is None else jax.ShapeDtypeStruct((T, o[0]), o[1]) for o in outs]
    return pl.pallas_call(
        functools.partial(_inproj_kernel, blocked_t=blocked_t),
        grid=(T // tm,),
        in_specs=[row(D_MODEL), full(g), full(w_pad)],
        out_specs=specs,
        out_shape=shapes,
        compiler_params=_cparams(("parallel",)),
        name="in_proj",
    )(x2d, g, w_pad)


def _rel_bucket_np(rel):
    half = REL_BUCKETS // 2
    max_exact = half // 2
    ret = np.where(rel > 0, half, 0)
    n = np.abs(rel)
    nf = np.maximum(n, 1).astype(np.float32)
    large = max_exact + (np.log(nf / np.float32(max_exact)) / np.float32(math.log(REL_MAX_DIST / max_exact))
                         * np.float32(half - max_exact)).astype(np.int32)
    large = np.minimum(large, half - 1)
    return (ret + np.where(n < max_exact, n, large)).astype(np.int32)


def _bias_kernel(tab_ref, idx_ref, o_ref, *, shift_bucket, scale):
    for n in range(idx_ref.shape[0]):
        idx = idx_ref[n]
        for h in range(DA_HEADS):
            acc = jnp.zeros(idx.shape, F32)
            for b in range(REL_BUCKETS):
                acc = jnp.where(idx == b, tab_ref[b, h], acc)
            if shift_bucket is not None:
                acc = acc - tab_ref[shift_bucket, h]
            o_ref[n, h] = acc * scale


def _bias_tiles(table, idx_np, shift_bucket, scale=1.0):
    n, r, c = idx_np.shape
    return pl.pallas_call(
        functools.partial(_bias_kernel, shift_bucket=shift_bucket, scale=scale),
        in_specs=[pl.BlockSpec(memory_space=pltpu.SMEM), pl.BlockSpec(memory_space=pltpu.VMEM)],
        out_specs=pl.BlockSpec(memory_space=pltpu.VMEM),
        out_shape=jax.ShapeDtypeStruct((n, DA_HEADS, r, c), F32),
        name="rel_bias",
    )(table, jnp.asarray(idx_np))


def _lambda(lam_ref, lam_init):
    lp = lam_ref[...]
    a = jnp.sum(lp[0:1] * lp[1:2], axis=-1, keepdims=True)
    b = jnp.sum(lp[2:3] * lp[3:4], axis=-1, keepdims=True)
    return jnp.exp(a) - jnp.exp(b) + lam_init


def _subln(o1, l1, o2, l2, lam, g, lam_init):
    o = o1 / l1 - lam * (o2 / l2)
    o = o * lax.rsqrt(jnp.mean(o * o, axis=-1, keepdims=True) + EPS) * g
    return o * (1.0 - lam_init)


def _attn_prompt_kernel(lam_ref, qt_ref, k_ref, vt_ref, bias_ref, g_ref, o_ref,
                        m_sc, l_sc, a_sc, acc_sc, s_sc, p_sc, *, lam_init):
    tb = ATT_BLOCK
    qb = pl.program_id(1)
    lam = _lambda(lam_ref, lam_init)
    krow = lax.broadcasted_iota(jnp.int32, (tb, tb), 0)
    qcol = lax.broadcasted_iota(jnp.int32, (tb, tb), 1)
    diag_mask = (krow // CHUNK) <= (qcol // CHUNK)
    comp0 = lax.broadcasted_iota(jnp.int32, (LANES, 1), 0) < DA_HEAD_DIM
    has_prev = qb >= 1

    m_sc[...] = jnp.full(m_sc.shape, NEG_INF, F32)
    l_sc[...] = jnp.zeros(l_sc.shape, F32)
    acc_sc[...] = jnp.zeros(acc_sc.shape, F32)

    qts = []
    for h in range(DA_HEADS):
        qt = qt_ref[0, LANES * h:LANES * (h + 1), :]
        qts.append((jnp.where(comp0, qt, jnp.zeros_like(qt)), jnp.where(comp0, jnp.zeros_like(qt), qt)))

    def sweep(kblk, bias_n=None, mask=None):
        for h in range(DA_HEADS):
            kh = k_ref[0, pl.ds(pl.multiple_of(kblk * tb, tb), tb), LANES * h:LANES * (h + 1)]
            for c in range(2):
                s = jnp.dot(kh, qts[h][c], preferred_element_type=F32)
                if bias_n is not None:
                    s = s + bias_ref[bias_n, h]
                if mask is not None:
                    s = jnp.where(mask, s, NEG_INF)
                s_sc[2 * h + c] = s
        for ch in range(2 * DA_HEADS):
            s = s_sc[ch]
            m_old = m_sc[ch]
            m_new = jnp.maximum(m_old, jnp.max(s, axis=0, keepdims=True))
            alpha = jnp.exp2(m_old - m_new)
            p = jnp.exp2(s - m_new[0:1])
            l_sc[ch] = alpha * l_sc[ch] + jnp.sum(p, axis=0, keepdims=True)
            m_sc[ch] = m_new
            a_sc[ch] = alpha
            p_sc[ch] = p.astype(BF16)
        for h in range(DA_HEADS):
            vth = vt_ref[kblk, LANES * h:LANES * (h + 1), :]
            for c in range(2):
                ch = 2 * h + c
                acc_sc[ch] = a_sc[ch][0:1] * acc_sc[ch] + jnp.dot(vth, p_sc[ch], preferred_element_type=F32)

    sweep(qb, 0, diag_mask)
    sweep(jnp.maximum(qb - 1, 0), 1, has_prev)
    lax.fori_loop(0, jnp.maximum(qb - 1, 0), lambda j, _: (sweep(j), 0)[1], 0)

    for h in range(DA_HEADS):
        o = acc_sc[2 * h] / l_sc[2 * h][0:1] - lam * (acc_sc[2 * h + 1] / l_sc[2 * h + 1][0:1])
        o = o * lax.rsqrt(jnp.mean(o * o, axis=0, keepdims=True) + EPS) * g_ref[...]
        o_ref[0, :, LANES * h:LANES * (h + 1)] = (o * (1.0 - lam_init)).T.astype(o_ref.dtype)


def _attn_prompt(qt, k, vt, bias, lam_p, g_t, lam_init, B):
    S = k.shape[1]
    tb = ATT_BLOCK
    nb = S // tb
    return pl.pallas_call(
        functools.partial(_attn_prompt_kernel, lam_init=lam_init),
        grid=(B, nb),
        in_specs=[
            pl.BlockSpec(lam_p.shape, lambda b, i: (0, 0)),
            pl.BlockSpec((1, DA_WIDTH, tb), lambda b, i: (b * nb + i, 0, 0)),
            pl.BlockSpec((1, S, DA_WIDTH), lambda b, i: (b, 0, 0)),
            pl.BlockSpec((nb, DA_WIDTH, tb), lambda b, i: (b, 0, 0)),
            pl.BlockSpec(bias.shape, lambda b, i: (0, 0, 0, 0)),
            pl.BlockSpec(g_t.shape, lambda b, i: (0, 0)),
        ],
        out_specs=pl.BlockSpec((1, tb, DA_WIDTH), lambda b, i: (b, i, 0)),
        out_shape=jax.ShapeDtypeStruct((B, S, DA_WIDTH), BF16),
        scratch_shapes=[pltpu.VMEM((2 * DA_HEADS, SUBLANES, tb), F32),
                        pltpu.VMEM((2 * DA_HEADS, SUBLANES, tb), F32),
                        pltpu.VMEM((2 * DA_HEADS, SUBLANES, tb), F32),
                        pltpu.VMEM((2 * DA_HEADS, LANES, tb), F32),
                        pltpu.VMEM((2 * DA_HEADS, tb, tb), F32),
                        pltpu.VMEM((2 * DA_HEADS, tb, tb), BF16)],
        compiler_params=_cparams(("parallel", "arbitrary")),
        name="attn_prompt",
    )(lam_p, qt, k, vt, bias, g_t)


def _attn_step_kernel(lam_ref, q_ref, kn_ref, vn_ref, kp_ref, vp_ref, bp_ref, bn_ref, g_ref, o_ref, *, lam_init):
    lam = _lambda(lam_ref, lam_init)
    first = lax.broadcasted_iota(jnp.int32, (1, LANES), 1) < DA_HEAD_DIM
    g = g_ref[...]
    for h in range(DA_HEADS):
        sl = slice(LANES * h, LANES * (h + 1))
        q = q_ref[0, :, sl]
        kn = kn_ref[0, :, sl]
        vn = vn_ref[0, :, sl]
        kp = kp_ref[0, :, sl].astype(BF16)
        vp = vp_ref[0, :, sl].astype(BF16)
        res = []
        for c in range(2):
            qc = jnp.where(first if c == 0 else jnp.logical_not(first), q, jnp.zeros_like(q))
            sp = _dot_nt(qc, kp) + bp_ref[0, h]
            sn = _dot_nt(qc, kn) + bn_ref[0, h]
            m = jnp.maximum(jnp.max(sp, axis=-1, keepdims=True), jnp.max(sn, axis=-1, keepdims=True))
            pp = jnp.exp(sp - m)
            pn = jnp.exp(sn - m)
            l = jnp.sum(pp, axis=-1, keepdims=True) + jnp.sum(pn, axis=-1, keepdims=True)
            acc = (jnp.dot(pp.astype(BF16), vp, preferred_element_type=F32)
                   + jnp.dot(pn.astype(BF16), vn, preferred_element_type=F32))
            res.append((l, acc))
        (l1, a1), (l2, a2) = res
        o_ref[0, :, sl] = _subln(a1, l1, a2, l2, lam, g, lam_init).astype(o_ref.dtype)


def _attn_step(q, kn, vn, kp, vp, bias_p, bias_n, lam_p, g, lam_init):
    B, L, _ = q.shape
    P = kp.shape[1]
    c3 = lambda b: (b, 0, 0)
    z4 = lambda b: (0, 0, 0, 0)
    return pl.pallas_call(
        functools.partial(_attn_step_kernel, lam_init=lam_init),
        grid=(B,),
        in_specs=[
            pl.BlockSpec(lam_p.shape, lambda b: (0, 0)),
            pl.BlockSpec((1, L, DA_WIDTH), c3), pl.BlockSpec((1, L, DA_WIDTH), c3), pl.BlockSpec((1, L, DA_WIDTH), c3),
            pl.BlockSpec((1, P, DA_WIDTH), c3), pl.BlockSpec((1, P, DA_WIDTH), c3),
            pl.BlockSpec(bias_p.shape, z4), pl.BlockSpec(bias_n.shape, z4),
            pl.BlockSpec(g.shape, lambda b: (0, 0)),
        ],
        out_specs=pl.BlockSpec((1, L, DA_WIDTH), c3),
        out_shape=jax.ShapeDtypeStruct((B, L, DA_WIDTH), BF16),
        compiler_params=_cparams(("parallel",)),
        name="attn_step",
    )(lam_p, q, kn, vn, kp, vp, bias_p, bias_n, g)


def _log_sigmoid(x):
    return jnp.minimum(x, 0.0) - jnp.log(1.0 + jnp.exp(-jnp.abs(x)))


def _mlstm_kernel(gb_ref, mc_ref, mv_ref, mo_ref, gc_ref, cw_ref, cb_ref, wq_ref, wk_ref, ng_ref, sk_ref,
                  c0_ref, n0_ref, m0_ref, cv0_ref,
                  o_ref, co_ref, no_ref, mo2_ref, cvo_ref,
                  xbuf, c_sc, n_sc, m_sc):
    j = pl.program_id(1)
    L = CHUNK
    bb = mc_ref.shape[0]
    chains = [(b, h) for b in range(bb) for h in range(ML_HEADS)]
    hsl = lambda h: slice(LANES * h, LANES * (h + 1))

    @pl.when(j == 0)
    def _():
        xbuf[:, 5:8, :] = cv0_ref[...]
        c_sc[...] = c0_ref[...]
        n_sc[...] = n0_ref[...]
        m_sc[...] = m0_ref[...]

    r = lax.broadcasted_iota(jnp.int32, (L, L), 0)
    s = lax.broadcasted_iota(jnp.int32, (L, L), 1)
    tril = s <= r

    cc, gcol, gt = [], [], []
    for b in range(bb):
        x = mc_ref[b]
        xbuf[b, 8:8 + L, :] = x
        y = (cb_ref[...] + cw_ref[3:4, :] * x + cw_ref[2:3, :] * xbuf[b, 7:7 + L, :]
             + cw_ref[1:2, :] * xbuf[b, 6:6 + L, :] + cw_ref[0:1, :] * xbuf[b, 5:5 + L, :])
        cc.append(y * jax.nn.sigmoid(y))
        tail = xbuf[b, 5 + L:8 + L, :]
        xbuf[b, 5:8, :] = tail
        cvo_ref[b] = tail
        g = gc_ref[b]
        gcol.append(g)
        gt.append(jnp.concatenate([g, jnp.zeros((LANES - L, LANES), F32)], axis=0).T)

    qs, ks = {}, {}
    for b, h in chains:
        cch = cc[b][:, hsl(h)]
        qs[b, h] = jnp.dot(cch, wq_ref[h], preferred_element_type=F32)
        ks[b, h] = jnp.dot(cch, wk_ref[h], preferred_element_type=F32) * (ML_HEAD_DIM ** -0.5)
    sc, rd = {}, {}
    for b, h in chains:
        sc[b, h] = _dot_nt(qs[b, h], ks[b, h])
        rd[b, h] = jnp.dot(qs[b, h], c_sc[b, h], preferred_element_type=F32)
    ig_c, ig_r, lf_c, lf_r = {}, {}, {}, {}
    for b, h in chains:
        b_i = gb_ref[0, h]
        b_f = gb_ref[1, h]
        ig_c[b, h] = gcol[b][:, h:h + 1] + b_i
        lf_c[b, h] = _log_sigmoid(gcol[b][:, 4 + h:5 + h] + b_f)
        ig_r[b, h] = gt[b][h:h + 1, 0:L] + b_i
        lf_r[b, h] = _log_sigmoid(gt[b][4 + h:5 + h, 0:L] + b_f)
    f_c, f_r = {}, {}
    for ch in chains:
        f_c[ch] = jnp.sum(jnp.where(tril, lf_r[ch], 0.0), axis=1, keepdims=True)
        f_r[ch] = jnp.sum(jnp.where(r <= s, lf_c[ch], 0.0), axis=0, keepdims=True)
    dmat, dmax, tmax, m_prev, fl = {}, {}, {}, {}, {}
    for b, h in chains:
        ch = (b, h)
        m_prev[ch] = m_sc[b, h][0:1, 0:1]
        fl[ch] = f_c[ch][L - 1:L, :]
        dmat[ch] = jnp.where(tril, f_c[ch] - f_r[ch] + ig_r[ch], NEG_INF)
        dmax[ch] = jnp.max(dmat[ch], axis=1, keepdims=True)
        tmax[ch] = jnp.max(fl[ch] - f_r[ch] + ig_r[ch], axis=1, keepdims=True)
    sw, iw, m_t, kw, wc, m_new = {}, {}, {}, {}, {}, {}
    for ch in chains:
        inter = f_c[ch] + m_prev[ch]
        m_t[ch] = jnp.maximum(inter, dmax[ch])
        sw[ch] = sc[ch] * jnp.exp(dmat[ch] - m_t[ch])
        iw[ch] = jnp.exp(inter - m_t[ch])
        m_new[ch] = jnp.maximum(fl[ch] + m_prev[ch], tmax[ch])
        wc[ch] = jnp.exp(fl[ch] + m_prev[ch] - m_new[ch])
        kw[ch] = ks[ch] * jnp.exp(fl[ch] - f_c[ch] + ig_c[ch] - m_new[ch])
    nv, up = {}, {}
    for b, h in chains:
        vh = mv_ref[b, :, hsl(h)]
        nv[b, h] = jnp.dot(sw[b, h], vh, preferred_element_type=F32)
        up[b, h] = _dot_tn(kw[b, h], vh)
    n_prev, den, ksum = {}, {}, {}
    for b, h in chains:
        ch = (b, h)
        n_prev[ch] = n_sc[b, h]
        den[ch] = (jnp.sum(sw[ch], axis=1, keepdims=True)
                   + iw[ch] * jnp.sum(qs[ch] * n_prev[ch], axis=1, keepdims=True))
        ksum[ch] = jnp.sum(kw[ch], axis=0, keepdims=True)
    hout, ssq = {}, {}
    for ch in chains:
        hout[ch] = (nv[ch] + iw[ch] * rd[ch]) / jnp.maximum(jnp.abs(den[ch]), jnp.exp(-m_t[ch]))
        ssq[ch] = jnp.sum(hout[ch] * hout[ch], axis=1, keepdims=True)
    for b, h in chains:
        ch = (b, h)
        c_new = wc[ch] * c_sc[b, h] + up[ch]
        n_new = wc[ch] * n_prev[ch] + ksum[ch]
        m_row = jnp.broadcast_to(m_new[ch], (1, LANES))
        c_sc[b, h] = c_new
        n_sc[b, h] = n_new
        m_sc[b, h] = m_row
        co_ref[b, h] = c_new
        no_ref[b, h] = n_new
        mo2_ref[b, h] = m_row
        hn = hout[ch] * lax.rsqrt(ssq[ch] * (1.0 / ML_HEAD_DIM) + EPS) * ng_ref[:, hsl(h)]
        o_ref[b, :, hsl(h)] = ((hn + sk_ref[:, hsl(h)] * cc[b][:, hsl(h)])
                               * jax.nn.sigmoid(mo_ref[b, :, hsl(h)])).astype(o_ref.dtype)


def _mlstm(gate_b, mc, mv, mo, gc, cw, cb, wq, wk, ng, sk, c0, n0, m0, cv0, bb):
    B, L, _ = mc.shape
    nc = L // CHUNK
    tok = lambda n: pl.BlockSpec((bb, CHUNK, n), lambda b, j: (b, j, 0))
    full = lambda a: pl.BlockSpec(a.shape, lambda b, j: (0,) * a.ndim)
    st4 = lambda a: pl.BlockSpec((bb,) + a.shape[1:], lambda b, j: (b,) + (0,) * (a.ndim - 1))
    return pl.pallas_call(
        _mlstm_kernel,
        grid=(B // bb, nc),
        in_specs=[pl.BlockSpec(memory_space=pltpu.SMEM),
                  tok(ML_PAD), tok(ML_PAD), tok(ML_PAD), tok(LANES),
                  full(cw), full(cb), full(wq), full(wk), full(ng), full(sk),
                  st4(c0), st4(n0), st4(m0), st4(cv0)],
        out_specs=[tok(ML_PAD), st4(c0), st4(n0), st4(m0), st4(cv0)],
        out_shape=[jax.ShapeDtypeStruct((B, L, ML_PAD), BF16),
                   jax.ShapeDtypeStruct(c0.shape, F32), jax.ShapeDtypeStruct(n0.shape, F32),
                   jax.ShapeDtypeStruct(m0.shape, F32), jax.ShapeDtypeStruct(cv0.shape, F32)],
        scratch_shapes=[pltpu.VMEM((bb, 8 + CHUNK, ML_PAD), F32),
                        pltpu.VMEM((bb, ML_HEADS, LANES, LANES), F32),
                        pltpu.VMEM((bb, ML_HEADS, 1, LANES), F32),
                        pltpu.VMEM((bb, ML_HEADS, 1, LANES), F32)],
        compiler_params=_cparams(("parallel", "arbitrary")),
        name="mlstm",
    )(gate_b, mc, mv, mo, gc, cw, cb, wq, wk, ng, sk, c0, n0, m0, cv0)


def _mixout_kernel(x_ref, oda_ref, oml_ref, cu_ref, cv_ref, cmg_ref, ws_ref, bm_ref,
                   wda_ref, wml_ref, wcm_ref, x1_ref, vcm_ref, *, lc):
    tm = x_ref.shape[0]
    u = _gelu(cu_ref[...])
    gv = _gelu(cv_ref[...])
    vcm = gv * lax.rsqrt(jnp.mean(gv * gv, axis=-1, keepdims=True) + EPS) * cmg_ref[...]
    vcm_ref[...] = vcm
    r = lax.broadcasted_iota(jnp.int32, (lc, lc), 0)
    c = lax.broadcasted_iota(jnp.int32, (lc, lc), 1)
    grp = lax.broadcasted_iota(jnp.int32, (1, CM_WIDTH), 1) // CM_GROUP_DIM
    wsm = [jnp.where(c <= r, ws_ref[g, 0:lc, 0:lc], 0.0).astype(BF16) for g in range(CM_GROUPS)]
    pieces = []
    for ci in range(tm // lc):
        vch = vcm[ci * lc:(ci + 1) * lc].astype(BF16)
        mixed = bm_ref[0:lc, :]
        for g in range(CM_GROUPS):
            mixed = mixed + jnp.where(grp == g, jnp.dot(wsm[g], vch, preferred_element_type=F32), 0.0)
        pieces.append(u[ci * lc:(ci + 1) * lc] * mixed)
    ocm = jnp.concatenate(pieces, axis=0) if len(pieces) > 1 else pieces[0]
    y = (jnp.dot(oda_ref[...], wda_ref[...], preferred_element_type=F32)
         + jnp.dot(oml_ref[...], wml_ref[...], preferred_element_type=F32)
         + jnp.dot(ocm.astype(BF16), wcm_ref[...], preferred_element_type=F32))
    x1_ref[...] = x_ref[...] + y


def _mix_out(x2d, oda, oml, cu, cv, cmg, ws, bmat, wda, wml, wcm, tm, lc):
    T = x2d.shape[0]
    row = lambda n: pl.BlockSpec((tm, n), lambda i: (i, 0))
    full = lambda a: pl.BlockSpec(a.shape, lambda i: (0,) * a.ndim)
    return pl.pallas_call(
        functools.partial(_mixout_kernel, lc=lc),
        grid=(T // tm,),
        in_specs=[row(D_MODEL), row(DA_WIDTH), row(ML_PAD), row(CM_WIDTH), row(CM_WIDTH),
                  full(cmg), full(ws), full(bmat), full(wda), full(wml), full(wcm)],
        out_specs=[row(D_MODEL), row(CM_WIDTH)],
        out_shape=[jax.ShapeDtypeStruct((T, D_MODEL), F32), jax.ShapeDtypeStruct((T, CM_WIDTH), F32)],
        compiler_params=_cparams(("parallel",)),
        name="mix_out",
    )(x2d, oda, oml, cu, cv, cmg, ws, bmat, wda, wml, wcm)


def _peer_query_kernel(x_ref, g_ref, wq_ref, keys_ref, xnt_ref, st_ref):
    x = x_ref[...]
    xf = x * lax.rsqrt(jnp.mean(x * x, axis=-1, keepdims=True) + EPS) * g_ref[...]
    xn = xf.astype(BF16)
    xnt_ref[...] = xf.T.astype(BF16)
    qs = [jnp.dot(xn, wq_ref[:, PEER_QDIM * h:PEER_QDIM * (h + 1)], preferred_element_type=F32)
          for h in range(PEER_HEADS)]
    qs = [(q * lax.rsqrt(jnp.mean(q * q, axis=-1, keepdims=True) + EPS)).astype(BF16) for q in qs]
    for h in range(PEER_HEADS):
        for c in range(2):
            qc = qs[h][:, PEER_HALF * c:PEER_HALF * (c + 1)]
            st_ref[h, c] = _dot_nt(keys_ref[h, c], qc)


def _peer_query(x1, g, wq, keys, tm):
    T = x1.shape[0]
    full = lambda a: pl.BlockSpec(a.shape, lambda i: (0,) * a.ndim)
    return pl.pallas_call(
        _peer_query_kernel,
        grid=(T // tm,),
        in_specs=[pl.BlockSpec((tm, D_MODEL), lambda i: (i, 0)), full(g), full(wq), full(keys)],
        out_specs=[pl.BlockSpec((D_MODEL, tm), lambda i: (0, i)),
                   pl.BlockSpec((PEER_HEADS, 2, PEER_KEYS, tm), lambda i: (0, 0, 0, i))],
        out_shape=[jax.ShapeDtypeStruct((D_MODEL, T), BF16),
                   jax.ShapeDtypeStruct((PEER_HEADS, 2, PEER_KEYS, T), F32)],
        compiler_params=_cparams(("parallel",)),
        name="peer_query",
    )(x1, g, wq, keys)


_NTOP = PEER_TOPK + 1
_PAIRS = [(a, b) for a in range(_NTOP) for b in range(_NTOP) if (a + 1) * (b + 1) <= _NTOP]
_NCAND = ((len(_PAIRS) + 7) // 8) * 8


def _peer_select_kernel(st_ref, a_ref, b_ref, bd_ref, vals, cand):
    tt = st_ref.shape[-1]
    for h in range(PEER_HEADS):
        for c in range(2):
            cur = st_ref[h, c]
            for i in range(_NTOP):
                m = jnp.max(cur, axis=0, keepdims=True)
                vals[c, i:i + 1, :] = m
                cur = jnp.where(cur == m, NEG_INF, cur)
        cand[...] = jnp.full((_NCAND, tt), NEG_INF, F32)
        for i, (a, b) in enumerate(_PAIRS):
            cand[i:i + 1, :] = vals[0, a:a + 1, :] + vals[1, b:b + 1, :]
        top1 = vals[0, 0:1, :]
        top2 = vals[1, 0:1, :]
        best = top1 + top2
        cur = cand[...]
        z = jnp.zeros((1, tt), F32)
        c16 = best
        for i in range(PEER_TOPK):
            c16 = jnp.max(cur, axis=0, keepdims=True)
            z = z + jnp.exp(c16 - best)
            cur = jnp.where(cur == c16, NEG_INF, cur)
        thr = 0.5 * (c16 + jnp.max(cur, axis=0, keepdims=True))
        s1 = st_ref[h, 0]
        a_ref[h] = 0.5 * jnp.exp(s1 - top1) / z
        b_ref[h] = jnp.exp(st_ref[h, 1] - top2)
        bd_ref[h] = jnp.exp(thr - s1 - top2)


def _peer_select(st, tt):
    T = st.shape[-1]
    spec = pl.BlockSpec((PEER_HEADS, PEER_KEYS, tt), lambda i: (0, 0, i))
    shp = jax.ShapeDtypeStruct((PEER_HEADS, PEER_KEYS, T), F32)
    return pl.pallas_call(
        _peer_select_kernel,
        grid=(T // tt,),
        in_specs=[pl.BlockSpec((PEER_HEADS, 2, PEER_KEYS, tt), lambda i: (0, 0, 0, i))],
        out_specs=[spec, spec, spec],
        out_shape=[shp, shp, shp],
        scratch_shapes=[pltpu.VMEM((2, 3 * SUBLANES, tt), F32), pltpu.VMEM((_NCAND, tt), F32)],
        compiler_params=_cparams(("parallel",)),
        name="peer_select",
    )(st)


def _peer_dense_kernel(x1_ref, xnt_ref, a_ref, b_ref, bd_ref, u_ref, vt_ref, fg_ref, o_ref,
                       acc_ref, ht_ref, at_ref, *, rows, final_norm):
    j = pl.program_id(1)
    last = pl.num_programs(1) - 1

    @pl.when(j == 0)
    def _():
        acc_ref[...] = jnp.zeros_like(acc_ref)

    per = MXU_N // PEER_KEYS
    nchunk = rows // per
    tt = ht_ref.shape[1]
    lw = LANES
    assert rows % SUBLANES == 0 and SUBLANES % per == 0

    def proj(c):
        cs = slice(c * MXU_N, (c + 1) * MXU_N)
        ht_ref[cs, :] = jnp.dot(u_ref[cs, :], xnt_ref[...], preferred_element_type=F32)

    def gate(c):
        rr = range(c * per, (c + 1) * per)
        for lt in range(tt // lw):
            ls = slice(lt * lw, (lt + 1) * lw)
            i0 = pl.multiple_of(j * rows + rr[0] // SUBLANES * SUBLANES, SUBLANES)
            att = [a_ref[h, pl.ds(i0, SUBLANES), ls] for h in range(PEER_HEADS)]
            bdt = [bd_ref[h, pl.ds(i0, SUBLANES), ls] for h in range(PEER_HEADS)]
            sub = [r % SUBLANES for r in rr]
            av = [[jnp.broadcast_to(att[h][q:q + 1], (SUBLANES, lw)) for h in range(PEER_HEADS)] for q in sub]
            bdv = [[jnp.broadcast_to(bdt[h][q:q + 1], (SUBLANES, lw)) for h in range(PEER_HEADS)] for q in sub]
            for sg in range(PEER_KEYS // (2 * SUBLANES)):
                halves = [[] for _ in rr]
                for half in range(2):
                    k0 = sg * 2 * SUBLANES + half * SUBLANES
                    ks = slice(k0, k0 + SUBLANES)
                    bv = [b_ref[h, ks, ls] for h in range(PEER_HEADS)]
                    for ri, r in enumerate(rr):
                        t = [jnp.where(bv[h] >= bdv[ri][h], bv[h], 0.0) * av[ri][h] for h in range(PEER_HEADS)]
                        w = ((t[0] + t[1]) + (t[2] + t[3])) + ((t[4] + t[5]) + (t[6] + t[7]))
                        ht = ht_ref[r * PEER_KEYS + k0:r * PEER_KEYS + k0 + SUBLANES, ls]
                        halves[ri].append(w * (ht * (1.0 + lax.erf(ht * (2.0 ** -0.5)))))
                for ri in range(per):
                    o0 = ri * PEER_KEYS + sg * 2 * SUBLANES
                    at_ref[c, o0:o0 + 2 * SUBLANES, ls] = jnp.concatenate(halves[ri], axis=0).astype(BF16)

    def value(c):
        acc_ref[...] += jnp.dot(vt_ref[:, c * MXU_N:(c + 1) * MXU_N], at_ref[c], preferred_element_type=F32)

    proj(0)
    if nchunk > 1:
        proj(1)
    for c in range(nchunk):
        gate(c)
        value(c)
        if c + 2 < nchunk:
            proj(c + 2)

    @pl.when(j == last)
    def _():
        x2 = x1_ref[...] + acc_ref[...].T
        if final_norm:
            x2 = x2 * lax.rsqrt(jnp.mean(x2 * x2, axis=-1, keepdims=True) + EPS) * fg_ref[...]
        o_ref[...] = x2


def _peer_dense(x1, xn, a, b, bd, u_bf, vt_bf, fg, tt, eb, final_norm):
    T = x1.shape[0]
    rows = eb // PEER_KEYS
    ne = PEER_EXPERTS // eb
    hk = pl.BlockSpec((PEER_HEADS, PEER_KEYS, tt), lambda i, j: (0, 0, i))
    return pl.pallas_call(
        functools.partial(_peer_dense_kernel, rows=rows, final_norm=final_norm),
        grid=(T // tt, ne),
        in_specs=[pl.BlockSpec((tt, D_MODEL), lambda i, j: (i, 0)),
                  pl.BlockSpec((D_MODEL, tt), lambda i, j: (0, i)),
                  hk, hk, hk,
                  pl.BlockSpec((eb, D_MODEL), lambda i, j: (j, 0)),
                  pl.BlockSpec((D_MODEL, eb), lambda i, j: (0, j)),
                  pl.BlockSpec(fg.shape, lambda i, j: (0, 0))],
        out_specs=pl.BlockSpec((tt, D_MODEL), lambda i, j: (i, 0)),
        out_shape=jax.ShapeDtypeStruct((T, D_MODEL), F32),
        scratch_shapes=[pltpu.VMEM((D_MODEL, tt), F32), pltpu.VMEM((eb, tt), F32),
                        pltpu.VMEM((eb // MXU_N, MXU_N, tt), BF16)],
        compiler_params=_cparams(("parallel", "arbitrary")),
        name="peer_dense",
    )(x1, xn, a, b, bd, u_bf, vt_bf, fg)


def _pad_heads(w, axis=-1):
    w = jnp.moveaxis(w, axis, -1)
    lead = w.shape[:-1]
    w = w.reshape(lead + (ML_HEADS, ML_HEAD_DIM))
    w = jnp.pad(w, [(0, 0)] * len(lead) + [(0, 0), (0, LANES - ML_HEAD_DIM)])
    return jnp.moveaxis(w.reshape(lead + (ML_PAD,)), -1, axis)


def _layer_params(l, norm1_g, w_in, da_lambda, da_subln_g, ml_conv_w, ml_conv_b, ml_wq, ml_wk, ml_gate_b,
                  ml_norm_g, ml_skip, cm_norm_g, cm_ws, cm_b, w_out, norm2_g, peer_wq, peer_keys, peer_u, peer_v):
    o = np.cumsum((0, 512, 512, 512, 256, 256, 256, 4, 4, 256, 256))
    seg = lambda i: w_in[:, o[i]:o[i + 1]]
    gates = jnp.pad(jnp.concatenate([seg(6), seg(7)], axis=1), ((0, 0), (0, LANES - 2 * ML_HEADS)))
    w_pad = jnp.concatenate([seg(0), seg(1), seg(2), _pad_heads(seg(3)), _pad_heads(seg(4)), _pad_heads(seg(5)),
                             seg(8), seg(9), gates], axis=1).astype(BF16)
    pad_sq = lambda w: jnp.pad(w, ((0, 0), (0, LANES - ML_HEAD_DIM), (0, LANES - ML_HEAD_DIM)))
    return dict(
        lam_init=0.8 - 0.6 * math.exp(-0.3 * l),
        g1=norm1_g.reshape(1, D_MODEL), w_pad=w_pad,
        lam_p=da_lambda, subln_g=da_subln_g.reshape(1, DA_VDIM),
        cw=_pad_heads(ml_conv_w), cb=_pad_heads(ml_conv_b.reshape(1, ML_WIDTH)),
        wq=pad_sq(ml_wq), wk=pad_sq(ml_wk), gate_b=ml_gate_b,
        ng=_pad_heads(ml_norm_g.reshape(1, ML_WIDTH)), sk=_pad_heads(ml_skip.reshape(1, ML_WIDTH)),
        cmg=cm_norm_g.reshape(1, CM_WIDTH), ws=cm_ws,
        bmat=jnp.repeat(cm_b.T, CM_GROUP_DIM, axis=1),
        wda=w_out[:DA_WIDTH].astype(BF16),
        wml=_pad_heads(w_out[DA_WIDTH:DA_WIDTH + ML_WIDTH], axis=0).astype(BF16),
        wcm=w_out[DA_WIDTH + ML_WIDTH:].astype(BF16),
        g2=norm2_g.reshape(1, D_MODEL), pwq=peer_wq.astype(BF16), pkeys=peer_keys.astype(BF16),
        u_bf=peer_u.astype(BF16), vt_bf=peer_v.T.astype(BF16),
    )


def _pick(n, prefs):
    for p in prefs:
        if n % p == 0:
            return p
    return n


def _layer(x, past, p, bias, final_g, final_norm):
    B, L, _ = x.shape
    T = B * L
    x2d = x.reshape(T, D_MODEL)
    tm = _pick(T, (512, 256, 128))
    q, kf, vf, kb, vb, mc, mv, mo, cu, cv, gc = _in_proj(x2d, p["g1"], p["w_pad"], tm, past is None)
    r3 = lambda a: a.reshape(B, L, a.shape[-1])
    if past is None:
        g_t = jnp.broadcast_to(p["subln_g"].reshape(DA_VDIM, 1), (DA_VDIM, ATT_BLOCK))
        oda = _attn_prompt(q, r3(kb), vb, bias, p["lam_p"], g_t, p["lam_init"], B)
        c0 = jnp.zeros((B, ML_HEADS, LANES, LANES), F32)
        n0 = jnp.zeros((B, ML_HEADS, 1, LANES), F32)
        m0 = jnp.zeros((B, ML_HEADS, 1, LANES), F32)
        cv0 = jnp.zeros((B, ML_CONV - 1, ML_PAD), F32)
        lc = CM_CHUNK
    else:
        pk, pv, pc, pn, pm, pconv = past
        P = pk.shape[1]
        oda = _attn_step(r3(q), r3(kb), r3(vb), pk.reshape(B, P, DA_WIDTH), pv.reshape(B, P, DA_WIDTH),
                         bias[0], bias[1], p["lam_p"], p["subln_g"], p["lam_init"])
        c0 = jnp.pad(pc, ((0, 0), (0, 0), (0, LANES - ML_HEAD_DIM), (0, LANES - ML_HEAD_DIM)))
        n0 = jnp.pad(pn, ((0, 0), (0, 0), (0, LANES - ML_HEAD_DIM)))[:, :, None, :]
        m0 = jnp.broadcast_to(pm[:, :, None, None], (B, ML_HEADS, 1, LANES))
        cv0 = _pad_heads(pconv)
        lc = L
    oml, c_new, n_new, m_new, conv_new = _mlstm(
        p["gate_b"], r3(mc), r3(mv), r3(mo), r3(gc), p["cw"], p["cb"], p["wq"], p["wk"], p["ng"], p["sk"],
        c0, n0, m0, cv0, ML_SEQS_PER_STEP)
    x1, vcm = _mix_out(x2d, oda.reshape(T, DA_WIDTH), oml.reshape(T, ML_PAD), cu, cv, p["cmg"], p["ws"], p["bmat"],
                       p["wda"], p["wml"], p["wcm"], _pick(T, (256, 128)), lc)
    xn2, st = _peer_query(x1, p["g2"], p["pwq"], p["pkeys"], _pick(T, (256, 128)))
    a, b, bd = _peer_select(st, _pick(T, (128,)))
    x2 = _peer_dense(x1, xn2, a, b, bd, p["u_bf"], p["vt_bf"], final_g, _pick(T, (512, 256, 128)), PEER_BLOCK_EXPERTS, final_norm)
    state = (kf.reshape(B, L, DA_HEADS, 2 * DA_HEAD_DIM), vf.reshape(B, L, DA_HEADS, DA_VDIM),
             c_new[:, :, :ML_HEAD_DIM, :ML_HEAD_DIM], n_new[:, :, 0, :ML_HEAD_DIM], m_new[:, :, 0, 0],
             conv_new.reshape(B, ML_CONV - 1, ML_HEADS, LANES)[..., :ML_HEAD_DIM].reshape(B, ML_CONV - 1, ML_WIDTH),
             vcm.reshape(B, L, CM_WIDTH))
    return x2.reshape(B, L, D_MODEL), state


def kernel(x_prompt, x_sample, cache_k, cache_v, state_mlstm_c, state_mlstm_n, state_mlstm_m, state_mlstm_conv, norm1_g, w_in, da_lambda, da_subln_g, rel_bias_table, ml_conv_w, ml_conv_b, ml_wq, ml_wk, ml_gate_b, ml_norm_g, ml_skip, cm_norm_g, cm_ws, cm_b, w_out, norm2_g, peer_wq, peer_keys, peer_u, peer_v, final_g):
    depth = w_in.shape[0]
    S = x_prompt.shape[1]
    Ld = x_sample.shape[1]
    P = cache_k.shape[2]
    assert S % ATT_BLOCK == 0
    far_bucket = int(_rel_bucket_np(np.array(-(ATT_BLOCK + 1))))
    ii = np.arange(ATT_BLOCK)
    idx_prompt = np.stack([_rel_bucket_np(ii[:, None] - ii[None, :]),
                           _rel_bucket_np(ii[:, None] - ATT_BLOCK - ii[None, :])])
    bias_prompt = _bias_tiles(rel_bias_table, idx_prompt, far_bucket, LOG2E)
    qpos = P + np.arange(Ld)
    bias_past = _bias_tiles(rel_bias_table, _rel_bucket_np(np.arange(P)[None, :] - qpos[:, None])[None], None)
    bias_new = _bias_tiles(rel_bias_table, _rel_bucket_np(qpos[None, :] - qpos[:, None])[None], None)
    fg = final_g.reshape(1, D_MODEL)

    hp, hs = x_prompt, x_sample
    st_p, st_s = [], []
    for l in range(depth):
        p = _layer_params(l, norm1_g[l], w_in[l], da_lambda[l], da_subln_g[l], ml_conv_w[l], ml_conv_b[l],
                          ml_wq[l], ml_wk[l], ml_gate_b[l], ml_norm_g[l], ml_skip[l], cm_norm_g[l], cm_ws[l],
                          cm_b[l], w_out[l], norm2_g[l], peer_wq[l], peer_keys[l], peer_u[l], peer_v[l])
        last = l == depth - 1
        hp, sp = _layer(hp, None, p, bias_prompt, fg, last)
        past = (cache_k[l], cache_v[l], state_mlstm_c[l], state_mlstm_n[l], state_mlstm_m[l], state_mlstm_conv[l])
        hs, ss = _layer(hs, past, p, (bias_past, bias_new), fg, last)
        st_p.append(sp)
        st_s.append(ss)
    stack = lambda sts, i: jnp.stack([s[i] for s in sts])
    return (hp, hs,
            stack(st_p, 0), stack(st_p, 1), stack(st_p, 2), stack(st_p, 3), stack(st_p, 4), stack(st_p, 5),
            stack(st_s, 0), stack(st_s, 1), stack(st_s, 2), stack(st_s, 3), stack(st_s, 4), stack(st_s, 5),
            stack(st_s, 6))
```

```python
import functools
import math

import numpy as np
import jax
import jax.numpy as jnp
from jax import lax
from jax.experimental import pallas as pl
from jax.experimental.pallas import tpu as pltpu

F32 = jnp.float32
BF16 = jnp.bfloat16

D_MODEL = 1024
CHUNK = 64
DA_HEADS = 4
DA_HEAD_DIM = 64
DA_VDIM = 128
DA_WIDTH = 512
ML_HEADS = 4
ML_HEAD_DIM = 64
ML_WIDTH = 256
ML_CONV = 4
CM_GROUPS = 4
CM_WIDTH = 256
CM_GROUP_DIM = 64
CM_CHUNK = 128
PEER_HEADS = 8
PEER_KEYS = 128
PEER_EXPERTS = PEER_KEYS * PEER_KEYS
PEER_QDIM = 256
PEER_HALF = 128
PEER_TOPK = 16
REL_BUCKETS = 32
REL_MAX_DIST = 128
EPS = 1e-6
NEG_INF = -1e30

LANES = 128
SUBLANES = 8
MXU_N = 256
ATT_BLOCK = MXU_N
FAR_BLOCKS = 2
LOG2E = 1.4426950408889634
ML_SEQS_PER_STEP = 4
ML_PAD = ML_HEADS * LANES
VMEM_LIMIT = 56 * 1024 * 1024
PEER_BLOCK_EXPERTS = 2048
PEER_CHUNK_EXPERTS = 512
PEER_ROWS_PER_SLAB = 2

_C_Q, _C_K, _C_V = 0, 512, 1024
_C_MC, _C_MV, _C_MO = 1536, 2048, 2560
_C_CU, _C_CV, _C_G = 3072, 3328, 3584
IN_PAD = 3712


def _cparams(sem):
    return pltpu.CompilerParams(dimension_semantics=sem, vmem_limit_bytes=VMEM_LIMIT)


def _gelu(x):
    return 0.5 * x * (1.0 + lax.erf(x * (2.0 ** -0.5)))


def _dot_nt(a, b):
    return lax.dot_general(a, b, (((1,), (1,)), ((), ())), preferred_element_type=F32)


def _dot_tn(a, b):
    return lax.dot_general(a, b, (((0,), (0,)), ((), ())), preferred_element_type=F32)


def _inproj_kernel(x_ref, g_ref, w_ref, q_ref, kf_ref, vf_ref, kb_ref, vb_ref,
                   mc_ref, mv_ref, mo_ref, cu_ref, cv_ref, gc_ref, *, blocked_t):
    x = x_ref[...]
    xn = x * lax.rsqrt(jnp.mean(x * x, axis=-1, keepdims=True) + EPS) * g_ref[...]
    xb = xn.astype(BF16)

    def proj(lo, hi):
        return jnp.dot(xb, w_ref[:, lo:hi], preferred_element_type=F32)

    def put(ref, val, scale=None):
        if scale is not None:
            val = val * scale
        if blocked_t:
            for n in range(ref.shape[0]):
                ref[n] = val[n * ATT_BLOCK:(n + 1) * ATT_BLOCK].T.astype(BF16)
        else:
            ref[...] = val.astype(BF16)

    put(q_ref, proj(_C_Q, _C_K), DA_HEAD_DIM ** -0.5 * (LOG2E if blocked_t else 1.0))
    k = proj(_C_K, _C_V)
    kf_ref[...] = k
    kb_ref[...] = k.astype(BF16)
    v = proj(_C_V, _C_MC)
    vf_ref[...] = v
    put(vb_ref, v)
    mc_ref[...] = proj(_C_MC, _C_MV)
    mv_ref[...] = proj(_C_MV, _C_MO)
    mo_ref[...] = proj(_C_MO, _C_CU)
    cu_ref[...] = proj(_C_CU, _C_CV)
    cv_ref[...] = proj(_C_CV, _C_G)
    gc_ref[...] = proj(_C_G, IN_PAD)


def _in_proj(x2d, g, w_pad, tm, blocked_t):
    T = x2d.shape[0]
    row = lambda n: pl.BlockSpec((tm, n), lambda i: (i, 0))
    full = lambda a: pl.BlockSpec(a.shape, lambda i: (0,) * a.ndim)
    if blocked_t:
        tspec = pl.BlockSpec((tm // ATT_BLOCK, DA_WIDTH, ATT_BLOCK), lambda i: (i, 0, 0))
        tshape = jax.ShapeDtypeStruct((T // ATT_BLOCK, DA_WIDTH, ATT_BLOCK), BF16)
    else:
        tspec, tshape = row(DA_WIDTH), jax.ShapeDtypeStruct((T, DA_WIDTH), BF16)
    outs = [(DA_WIDTH, F32), (DA_WIDTH, F32), (DA_WIDTH, BF16), None,
            (ML_PAD, F32), (ML_PAD, F32), (ML_PAD, F32), (CM_WIDTH, F32), (CM_WIDTH, F32), (LANES, F32)]
    specs = [tspec] + [tspec if o is None else row(o[0]) for o in outs]
    shapes = [tshape] + [tshape if o is None else jax.ShapeDtypeStruct((T, o[0]), o[1]) for o in outs]
    return pl.pallas_call(
        functools.partial(_inproj_kernel, blocked_t=blocked_t),
        grid=(T // tm,),
        in_specs=[row(D_MODEL), full(g), full(w_pad)],
        out_specs=specs,
        out_shape=shapes,
        compiler_params=_cparams(("parallel",)),
        name="in_proj",
    )(x2d, g, w_pad)


def _rel_bucket_np(rel):
    half = REL_BUCKETS // 2
    max_exact = half // 2
    ret = np.where(rel > 0, half, 0)
    n = np.abs(rel)
    nf = np.maximum(n, 1).astype(np.float32)
    large = max_exact + (np.log(nf / np.float32(max_exact)) / np.float32(math.log(REL_MAX_DIST / max_exact))
                         * np.float32(half - max_exact)).astype(np.int32)
    large = np.minimum(large, half - 1)
    return (ret + np.where(n < max_exact, n, large)).astype(np.int32)


def _bias_kernel(tab_ref, idx_ref, o_ref, *, shift_bucket, scale):
    for n in range(idx_ref.shape[0]):
        idx = idx_ref[n]
        for h in range(DA_HEADS):
            acc = jnp.zeros(idx.shape, F32)
            for b in range(REL_BUCKETS):
                acc = jnp.where(idx == b, tab_ref[b, h], acc)
            if shift_bucket is not None:
                acc = acc - tab_ref[shift_bucket, h]
            o_ref[n, h] = acc * scale


def _bias_tiles(table, idx_np, shift_bucket, scale=1.0):
    n, r, c = idx_np.shape
    return pl.pallas_call(
        functools.partial(_bias_kernel, shift_bucket=shift_bucket, scale=scale),
        in_specs=[pl.BlockSpec(memory_space=pltpu.SMEM), pl.BlockSpec(memory_space=pltpu.VMEM)],
        out_specs=pl.BlockSpec(memory_space=pltpu.VMEM),
        out_shape=jax.ShapeDtypeStruct((n, DA_HEADS, r, c), F32),
        name="rel_bias",
    )(table, jnp.asarray(idx_np))


def _lambda(lam_ref, lam_init):
    lp = lam_ref[...]
    a = jnp.sum(lp[0:1] * lp[1:2], axis=-1, keepdims=True)
    b = jnp.sum(lp[2:3] * lp[3:4], axis=-1, keepdims=True)
    return jnp.exp(a) - jnp.exp(b) + lam_init


def _subln(o1, l1, o2, l2, lam, g, lam_init):
    o = o1 / l1 - lam * (o2 / l2)
    o = o * lax.rsqrt(jnp.mean(o * o, axis=-1, keepdims=True) + EPS) * g
    return o * (1.0 - lam_init)


def _attn_prompt_kernel(lam_ref, qt_ref, k_ref, vt_ref, bias_ref, g_ref, o_ref,
                        m_sc, l_sc, a_sc, acc_sc, s_sc, p_sc, *, lam_init):
    tb = ATT_BLOCK
    qb = pl.program_id(1)
    lam = _lambda(lam_ref, lam_init)
    krow = lax.broadcasted_iota(jnp.int32, (tb, tb), 0)
    qcol = lax.broadcasted_iota(jnp.int32, (tb, tb), 1)
    diag_mask = (krow // CHUNK) <= (qcol // CHUNK)
    comp0 = lax.broadcasted_iota(jnp.int32, (LANES, 1), 0) < DA_HEAD_DIM
    has_prev = qb >= 1

    m_sc[...] = jnp.full(m_sc.shape, NEG_INF, F32)
    l_sc[...] = jnp.zeros(l_sc.shape, F32)
    acc_sc[...] = jnp.zeros(acc_sc.shape, F32)

    qts = []
    for h in range(DA_HEADS):
        qt = qt_ref[0, LANES * h:LANES * (h + 1), :]
        qts.append((jnp.where(comp0, qt, jnp.zeros_like(qt)), jnp.where(comp0, jnp.zeros_like(qt), qt)))

    def sweep(kblk, bias_n=None, mask=None, nk=1):
        rk = slice(0, nk * tb)
        for h in range(DA_HEADS):
            kh = k_ref[0, pl.ds(pl.multiple_of(kblk * tb, tb), nk * tb), LANES * h:LANES * (h + 1)]
            for c in range(2):
                s = jnp.dot(kh, qts[h][c], preferred_element_type=F32)
                if bias_n is not None:
                    s = s + bias_ref[bias_n, h]
                if mask is not None:
                    s = jnp.where(mask, s, NEG_INF)
                s_sc[2 * h + c, rk] = s
        for ch in range(2 * DA_HEADS):
            s = s_sc[ch, rk]
            m_old = m_sc[ch]
            m_new = jnp.maximum(m_old, jnp.max(s, axis=0, keepdims=True))
            alpha = jnp.exp2(m_old - m_new)
            p = jnp.exp2(s - m_new[0:1])
            l_sc[ch] = alpha * l_sc[ch] + jnp.sum(p, axis=0, keepdims=True)
            m_sc[ch] = m_new
            a_sc[ch] = alpha
            p_sc[ch, rk] = p.astype(BF16)
        for h in range(DA_HEADS):
            vth = [vt_ref[kblk + n, LANES * h:LANES * (h + 1), :] for n in range(nk)]
            vth = vth[0] if nk == 1 else jnp.concatenate(vth, axis=1)
            for c in range(2):
                ch = 2 * h + c
                acc_sc[ch] = a_sc[ch][0:1] * acc_sc[ch] + jnp.dot(vth, p_sc[ch, rk], preferred_element_type=F32)

    sweep(qb, 0, diag_mask)
    sweep(jnp.maximum(qb - 1, 0), 1, has_prev)
    nfar = jnp.maximum(qb - 1, 0)
    lax.fori_loop(0, nfar // FAR_BLOCKS, lambda j, _: (sweep(FAR_BLOCKS * j, nk=FAR_BLOCKS), 0)[1], 0)
    lax.fori_loop(nfar // FAR_BLOCKS * FAR_BLOCKS, nfar, lambda j, _: (sweep(j), 0)[1], 0)

    for h in range(DA_HEADS):
        o = acc_sc[2 * h] / l_sc[2 * h][0:1] - lam * (acc_sc[2 * h + 1] / l_sc[2 * h + 1][0:1])
        o = o * lax.rsqrt(jnp.mean(o * o, axis=0, keepdims=True) + EPS) * g_ref[...]
        o_ref[0, :, LANES * h:LANES * (h + 1)] = (o * (1.0 - lam_init)).T.astype(o_ref.dtype)


def _attn_prompt(qt, k, vt, bias, lam_p, g_t, lam_init, B):
    S = k.shape[1]
    tb = ATT_BLOCK
    nb = S // tb
    return pl.pallas_call(
        functools.partial(_attn_prompt_kernel, lam_init=lam_init),
        grid=(B, nb),
        in_specs=[
            pl.BlockSpec(lam_p.shape, lambda b, i: (0, 0)),
            pl.BlockSpec((1, DA_WIDTH, tb), lambda b, i: (b * nb + i, 0, 0)),
            pl.BlockSpec((1, S, DA_WIDTH), lambda b, i: (b, 0, 0)),
            pl.BlockSpec((nb, DA_WIDTH, tb), lambda b, i: (b, 0, 0)),
            pl.BlockSpec(bias.shape, lambda b, i: (0, 0, 0, 0)),
            pl.BlockSpec(g_t.shape, lambda b, i: (0, 0)),
        ],
        out_specs=pl.BlockSpec((1, tb, DA_WIDTH), lambda b, i: (b, i, 0)),
        out_shape=jax.ShapeDtypeStruct((B, S, DA_WIDTH), BF16),
        scratch_shapes=[pltpu.VMEM((2 * DA_HEADS, SUBLANES, tb), F32),
                        pltpu.VMEM((2 * DA_HEADS, SUBLANES, tb), F32),
                        pltpu.VMEM((2 * DA_HEADS, SUBLANES, tb), F32),
                        pltpu.VMEM((2 * DA_HEADS, LANES, tb), F32),
                        pltpu.VMEM((2 * DA_HEADS, FAR_BLOCKS * tb, tb), F32),
                        pltpu.VMEM((2 * DA_HEADS, FAR_BLOCKS * tb, tb), BF16)],
        compiler_params=_cparams(("parallel", "arbitrary")),
        name="attn_prompt",
    )(lam_p, qt, k, vt, bias, g_t)


def _attn_step_kernel(lam_ref, q_ref, kn_ref, vn_ref, kp_ref, vp_ref, bp_ref, bn_ref, g_ref, o_ref, *, lam_init):
    lam = _lambda(lam_ref, lam_init)
    first = lax.broadcasted_iota(jnp.int32, (1, LANES), 1) < DA_HEAD_DIM
    g = g_ref[...]
    for h in range(DA_HEADS):
        sl = slice(LANES * h, LANES * (h + 1))
        q = q_ref[0, :, sl]
        kn = kn_ref[0, :, sl]
        vn = vn_ref[0, :, sl]
        kp = kp_ref[0, :, sl].astype(BF16)
        vp = vp_ref[0, :, sl].astype(BF16)
        res = []
        for c in range(2):
            qc = jnp.where(first if c == 0 else jnp.logical_not(first), q, jnp.zeros_like(q))
            sp = _dot_nt(qc, kp) + bp_ref[0, h]
            sn = _dot_nt(qc, kn) + bn_ref[0, h]
            m = jnp.maximum(jnp.max(sp, axis=-1, keepdims=True), jnp.max(sn, axis=-1, keepdims=True))
            pp = jnp.exp(sp - m)
            pn = jnp.exp(sn - m)
            l = jnp.sum(pp, axis=-1, keepdims=True) + jnp.sum(pn, axis=-1, keepdims=True)
            acc = (jnp.dot(pp.astype(BF16), vp, preferred_element_type=F32)
                   + jnp.dot(pn.astype(BF16), vn, preferred_element_type=F32))
            res.append((l, acc))
        (l1, a1), (l2, a2) = res
        o_ref[0, :, sl] = _subln(a1, l1, a2, l2, lam, g, lam_init).astype(o_ref.dtype)


def _attn_step(q, kn, vn, kp, vp, bias_p, bias_n, lam_p, g, lam_init):
    B, L, _ = q.shape
    P = kp.shape[1]
    c3 = lambda b: (b, 0, 0)
    z4 = lambda b: (0, 0, 0, 0)
    return pl.pallas_call(
        functools.partial(_attn_step_kernel, lam_init=lam_init),
        grid=(B,),
        in_specs=[
            pl.BlockSpec(lam_p.shape, lambda b: (0, 0)),
            pl.BlockSpec((1, L, DA_WIDTH), c3), pl.BlockSpec((1, L, DA_WIDTH), c3), pl.BlockSpec((1, L, DA_WIDTH), c3),
            pl.BlockSpec((1, P, DA_WIDTH), c3), pl.BlockSpec((1, P, DA_WIDTH), c3),
            pl.BlockSpec(bias_p.shape, z4), pl.BlockSpec(bias_n.shape, z4),
            pl.BlockSpec(g.shape, lambda b: (0, 0)),
        ],
        out_specs=pl.BlockSpec((1, L, DA_WIDTH), c3),
        out_shape=jax.ShapeDtypeStruct((B, L, DA_WIDTH), BF16),
        compiler_params=_cparams(("parallel",)),
        name="attn_step",
    )(lam_p, q, kn, vn, kp, vp, bias_p, bias_n, g)


def _log_sigmoid(x):
    return jnp.minimum(x, 0.0) - jnp.log(1.0 + jnp.exp(-jnp.abs(x)))


def _mlstm_kernel(gb_ref, mc_ref, mv_ref, mo_ref, gc_ref, cw_ref, cb_ref, wq_ref, wk_ref, ng_ref, sk_ref,
                  c0_ref, n0_ref, m0_ref, cv0_ref,
                  o_ref, co_ref, no_ref, mo2_ref, cvo_ref,
                  xbuf, c_sc, n_sc, m_sc):
    j = pl.program_id(1)
    L = CHUNK
    bb = mc_ref.shape[0]
    chains = [(b, h) for b in range(bb) for h in range(ML_HEADS)]
    hsl = lambda h: slice(LANES * h, LANES * (h + 1))

    @pl.when(j == 0)
    def _():
        xbuf[:, 5:8, :] = cv0_ref[...]
        c_sc[...] = c0_ref[...]
        n_sc[...] = n0_ref[...]
        m_sc[...] = m0_ref[...]

    r = lax.broadcasted_iota(jnp.int32, (L, L), 0)
    s = lax.broadcasted_iota(jnp.int32, (L, L), 1)
    tril = s <= r

    cc, gcol, gt = [], [], []
    for b in range(bb):
        x = mc_ref[b]
        xbuf[b, 8:8 + L, :] = x
        y = (cb_ref[...] + cw_ref[3:4, :] * x + cw_ref[2:3, :] * xbuf[b, 7:7 + L, :]
             + cw_ref[1:2, :] * xbuf[b, 6:6 + L, :] + cw_ref[0:1, :] * xbuf[b, 5:5 + L, :])
        cc.append(y * jax.nn.sigmoid(y))
        tail = xbuf[b, 5 + L:8 + L, :]
        xbuf[b, 5:8, :] = tail
        cvo_ref[b] = tail
        g = gc_ref[b]
        gcol.append(g)
        gt.append(jnp.concatenate([g, jnp.zeros((LANES - L, LANES), F32)], axis=0).T)

    qs, ks = {}, {}
    for b, h in chains:
        cch = cc[b][:, hsl(h)]
        qs[b, h] = jnp.dot(cch, wq_ref[h], preferred_element_type=F32)
        ks[b, h] = jnp.dot(cch, wk_ref[h], preferred_element_type=F32) * (ML_HEAD_DIM ** -0.5)
    sc, rd = {}, {}
    for b, h in chains:
        sc[b, h] = _dot_nt(qs[b, h], ks[b, h])
        rd[b, h] = jnp.dot(qs[b, h], c_sc[b, h], preferred_element_type=F32)
    ig_c, ig_r, lf_c, lf_r = {}, {}, {}, {}
    for b, h in chains:
        b_i = gb_ref[0, h]
        b_f = gb_ref[1, h]
        ig_c[b, h] = gcol[b][:, h:h + 1] + b_i
        lf_c[b, h] = _log_sigmoid(gcol[b][:, 4 + h:5 + h] + b_f)
        ig_r[b, h] = gt[b][h:h + 1, 0:L] + b_i
        lf_r[b, h] = _log_sigmoid(gt[b][4 + h:5 + h, 0:L] + b_f)
    f_c, f_r = {}, {}
    for ch in chains:
        f_c[ch] = jnp.sum(jnp.where(tril, lf_r[ch], 0.0), axis=1, keepdims=True)
        f_r[ch] = jnp.sum(jnp.where(r <= s, lf_c[ch], 0.0), axis=0, keepdims=True)
    dmat, dmax, tmax, m_prev, fl = {}, {}, {}, {}, {}
    for b, h in chains:
        ch = (b, h)
        m_prev[ch] = m_sc[b, h][0:1, 0:1]
        fl[ch] = f_c[ch][L - 1:L, :]
        dmat[ch] = jnp.where(tril, f_c[ch] - f_r[ch] + ig_r[ch], NEG_INF)
        dmax[ch] = jnp.max(dmat[ch], axis=1, keepdims=True)
        tmax[ch] = jnp.max(fl[ch] - f_r[ch] + ig_r[ch], axis=1, keepdims=True)
    sw, iw, m_t, kw, wc, m_new = {}, {}, {}, {}, {}, {}
    for ch in chains:
        inter = f_c[ch] + m_prev[ch]
        m_t[ch] = jnp.maximum(inter, dmax[ch])
        sw[ch] = sc[ch] * jnp.exp(dmat[ch] - m_t[ch])
        iw[ch] = jnp.exp(inter - m_t[ch])
        m_new[ch] = jnp.maximum(fl[ch] + m_prev[ch], tmax[ch])
        wc[ch] = jnp.exp(fl[ch] + m_prev[ch] - m_new[ch])
        kw[ch] = ks[ch] * jnp.exp(fl[ch] - f_c[ch] + ig_c[ch] - m_new[ch])
    nv, up = {}, {}
    for b, h in chains:
        vh = mv_ref[b, :, hsl(h)]
        nv[b, h] = jnp.dot(sw[b, h], vh, preferred_element_type=F32)
        up[b, h] = _dot_tn(kw[b, h], vh)
    n_prev, den, ksum = {}, {}, {}
    for b, h in chains:
        ch = (b, h)
        n_prev[ch] = n_sc[b, h]
        den[ch] = (jnp.sum(sw[ch], axis=1, keepdims=True)
                   + iw[ch] * jnp.sum(qs[ch] * n_prev[ch], axis=1, keepdims=True))
        ksum[ch] = jnp.sum(kw[ch], axis=0, keepdims=True)
    hout, ssq = {}, {}
    for ch in chains:
        hout[ch] = (nv[ch] + iw[ch] * rd[ch]) / jnp.maximum(jnp.abs(den[ch]), jnp.exp(-m_t[ch]))
        ssq[ch] = jnp.sum(hout[ch] * hout[ch], axis=1, keepdims=True)
    for b, h in chains:
        ch = (b, h)
        c_new = wc[ch] * c_sc[b, h] + up[ch]
        n_new = wc[ch] * n_prev[ch] + ksum[ch]
        m_row = jnp.broadcast_to(m_new[ch], (1, LANES))
        c_sc[b, h] = c_new
        n_sc[b, h] = n_new
        m_sc[b, h] = m_row
        co_ref[b, h] = c_new
        no_ref[b, h] = n_new
        mo2_ref[b, h] = m_row
        hn = hout[ch] * lax.rsqrt(ssq[ch] * (1.0 / ML_HEAD_DIM) + EPS) * ng_ref[:, hsl(h)]
        o_ref[b, :, hsl(h)] = ((hn + sk_ref[:, hsl(h)] * cc[b][:, hsl(h)])
                               * jax.nn.sigmoid(mo_ref[b, :, hsl(h)])).astype(o_ref.dtype)


def _mlstm(gate_b, mc, mv, mo, gc, cw, cb, wq, wk, ng, sk, c0, n0, m0, cv0, bb):
    B, L, _ = mc.shape
    nc = L // CHUNK
    tok = lambda n: pl.BlockSpec((bb, CHUNK, n), lambda b, j: (b, j, 0))
    full = lambda a: pl.BlockSpec(a.shape, lambda b, j: (0,) * a.ndim)
    st4 = lambda a: pl.BlockSpec((bb,) + a.shape[1:], lambda b, j: (b,) + (0,) * (a.ndim - 1))
    return pl.pallas_call(
        _mlstm_kernel,
        grid=(B // bb, nc),
        in_specs=[pl.BlockSpec(memory_space=pltpu.SMEM),
                  tok(ML_PAD), tok(ML_PAD), tok(ML_PAD), tok(LANES),
                  full(cw), full(cb), full(wq), full(wk), full(ng), full(sk),
                  st4(c0), st4(n0), st4(m0), st4(cv0)],
        out_specs=[tok(ML_PAD), st4(c0), st4(n0), st4(m0), st4(cv0)],
        out_shape=[jax.ShapeDtypeStruct((B, L, ML_PAD), BF16),
                   jax.ShapeDtypeStruct(c0.shape, F32), jax.ShapeDtypeStruct(n0.shape, F32),
                   jax.ShapeDtypeStruct(m0.shape, F32), jax.ShapeDtypeStruct(cv0.shape, F32)],
        scratch_shapes=[pltpu.VMEM((bb, 8 + CHUNK, ML_PAD), F32),
                        pltpu.VMEM((bb, ML_HEADS, LANES, LANES), F32),
                        pltpu.VMEM((bb, ML_HEADS, 1, LANES), F32),
                        pltpu.VMEM((bb, ML_HEADS, 1, LANES), F32)],
        compiler_params=_cparams(("parallel", "arbitrary")),
        name="mlstm",
    )(gate_b, mc, mv, mo, gc, cw, cb, wq, wk, ng, sk, c0, n0, m0, cv0)


def _mixout_kernel(x_ref, oda_ref, oml_ref, cu_ref, cv_ref, cmg_ref, ws_ref, bm_ref,
                   wda_ref, wml_ref, wcm_ref, x1_ref, vcm_ref, *, lc):
    tm = x_ref.shape[0]
    u = _gelu(cu_ref[...])
    gv = _gelu(cv_ref[...])
    vcm = gv * lax.rsqrt(jnp.mean(gv * gv, axis=-1, keepdims=True) + EPS) * cmg_ref[...]
    vcm_ref[...] = vcm
    r = lax.broadcasted_iota(jnp.int32, (lc, lc), 0)
    c = lax.broadcasted_iota(jnp.int32, (lc, lc), 1)
    grp = lax.broadcasted_iota(jnp.int32, (1, CM_WIDTH), 1) // CM_GROUP_DIM
    wsm = [jnp.where(c <= r, ws_ref[g, 0:lc, 0:lc], 0.0).astype(BF16) for g in range(CM_GROUPS)]
    pieces = []
    for ci in range(tm // lc):
        vch = vcm[ci * lc:(ci + 1) * lc].astype(BF16)
        mixed = bm_ref[0:lc, :]
        for g in range(CM_GROUPS):
            mixed = mixed + jnp.where(grp == g, jnp.dot(wsm[g], vch, preferred_element_type=F32), 0.0)
        pieces.append(u[ci * lc:(ci + 1) * lc] * mixed)
    ocm = jnp.concatenate(pieces, axis=0) if len(pieces) > 1 else pieces[0]
    y = (jnp.dot(oda_ref[...], wda_ref[...], preferred_element_type=F32)
         + jnp.dot(oml_ref[...], wml_ref[...], preferred_element_type=F32)
         + jnp.dot(ocm.astype(BF16), wcm_ref[...], preferred_element_type=F32))
    x1_ref[...] = x_ref[...] + y


def _mix_out(x2d, oda, oml, cu, cv, cmg, ws, bmat, wda, wml, wcm, tm, lc):
    T = x2d.shape[0]
    row = lambda n: pl.BlockSpec((tm, n), lambda i: (i, 0))
    full = lambda a: pl.BlockSpec(a.shape, lambda i: (0,) * a.ndim)
    return pl.pallas_call(
        functools.partial(_mixout_kernel, lc=lc),
        grid=(T // tm,),
        in_specs=[row(D_MODEL), row(DA_WIDTH), row(ML_PAD), row(CM_WIDTH), row(CM_WIDTH),
                  full(cmg), full(ws), full(bmat), full(wda), full(wml), full(wcm)],
        out_specs=[row(D_MODEL), row(CM_WIDTH)],
        out_shape=[jax.ShapeDtypeStruct((T, D_MODEL), F32), jax.ShapeDtypeStruct((T, CM_WIDTH), F32)],
        compiler_params=_cparams(("parallel",)),
        name="mix_out",
    )(x2d, oda, oml, cu, cv, cmg, ws, bmat, wda, wml, wcm)


def _peer_query_kernel(x_ref, g_ref, wq_ref, keys_ref, xnt_ref, st_ref):
    x = x_ref[...]
    xf = x * lax.rsqrt(jnp.mean(x * x, axis=-1, keepdims=True) + EPS) * g_ref[...]
    xn = xf.astype(BF16)
    xnt_ref[...] = xf.T.astype(BF16)
    qs = [jnp.dot(xn, wq_ref[:, PEER_QDIM * h:PEER_QDIM * (h + 1)], preferred_element_type=F32)
          for h in range(PEER_HEADS)]
    qs = [(q * lax.rsqrt(jnp.mean(q * q, axis=-1, keepdims=True) + EPS)).astype(BF16) for q in qs]
    for h in range(PEER_HEADS):
        for c in range(2):
            qc = qs[h][:, PEER_HALF * c:PEER_HALF * (c + 1)]
            st_ref[h, c] = _dot_nt(keys_ref[h, c], qc)


def _peer_query(x1, g, wq, keys, tm):
    T = x1.shape[0]
    full = lambda a: pl.BlockSpec(a.shape, lambda i: (0,) * a.ndim)
    return pl.pallas_call(
        _peer_query_kernel,
        grid=(T // tm,),
        in_specs=[pl.BlockSpec((tm, D_MODEL), lambda i: (i, 0)), full(g), full(wq), full(keys)],
        out_specs=[pl.BlockSpec((D_MODEL, tm), lambda i: (0, i)),
                   pl.BlockSpec((PEER_HEADS, 2, PEER_KEYS, tm), lambda i: (0, 0, 0, i))],
        out_shape=[jax.ShapeDtypeStruct((D_MODEL, T), BF16),
                   jax.ShapeDtypeStruct((PEER_HEADS, 2, PEER_KEYS, T), F32)],
        compiler_params=_cparams(("parallel",)),
        name="peer_query",
    )(x1, g, wq, keys)


_NTOP = PEER_TOPK + 1
_PAIRS = [(a, b) for a in range(_NTOP) for b in range(_NTOP) if (a + 1) * (b + 1) <= _NTOP]
_NCAND = ((len(_PAIRS) + 7) // 8) * 8


def _merge_exchange_network(n):
    pairs = []
    t = max(1, math.ceil(math.log2(n)))
    p = 1 << (t - 1)
    while p > 0:
        q, r, d = 1 << (t - 1), 0, p
        while d > 0:
            pairs += [(i, i + d) for i in range(n - d) if (i & p) == r]
            d, q, r = q - p, q >> 1, p
        p >>= 1
    return pairs


_KEY_SLABS = PEER_KEYS // SUBLANES
_SORT_NET = _merge_exchange_network(_KEY_SLABS)


def _peer_select_kernel(st_ref, a_ref, b_ref, bd_ref, vals, cand):
    tt = st_ref.shape[-1]
    for h in range(PEER_HEADS):
        for c in range(2):
            v = [st_ref[h, c, SUBLANES * j:SUBLANES * (j + 1), :] for j in range(_KEY_SLABS)]
            for i, j in _SORT_NET:
                v[i], v[j] = jnp.maximum(v[i], v[j]), jnp.minimum(v[i], v[j])
            for i in range(_NTOP):
                m = jnp.max(v[0], axis=0, keepdims=True)
                vals[c, i:i + 1, :] = m
                keep = _NTOP - 1 - i
                popped = v[0] == m
                for j in range(min(keep, _KEY_SLABS)):
                    v[j] = jnp.where(popped, v[j + 1] if j + 1 < _KEY_SLABS else NEG_INF, v[j])
        cand[...] = jnp.full((_NCAND, tt), NEG_INF, F32)
        for i, (a, b) in enumerate(_PAIRS):
            cand[i:i + 1, :] = vals[0, a:a + 1, :] + vals[1, b:b + 1, :]
        top1 = vals[0, 0:1, :]
        top2 = vals[1, 0:1, :]
        best = top1 + top2
        cur = cand[...]
        z = jnp.zeros((1, tt), F32)
        c16 = best
        for i in range(PEER_TOPK):
            c16 = jnp.max(cur, axis=0, keepdims=True)
            z = z + jnp.exp(c16 - best)
            cur = jnp.where(cur == c16, NEG_INF, cur)
        thr = 0.5 * (c16 + jnp.max(cur, axis=0, keepdims=True))
        s1 = st_ref[h, 0]
        a_ref[h] = 0.5 * jnp.exp(s1 - top1) / z
        b_ref[h] = jnp.exp(st_ref[h, 1] - top2)
        bd_ref[h] = jnp.exp(thr - s1 - top2)


def _peer_select(st, tt):
    T = st.shape[-1]
    spec = pl.BlockSpec((PEER_HEADS, PEER_KEYS, tt), lambda i: (0, 0, i))
    shp = jax.ShapeDtypeStruct((PEER_HEADS, PEER_KEYS, T), F32)
    return pl.pallas_call(
        _peer_select_kernel,
        grid=(T // tt,),
        in_specs=[pl.BlockSpec((PEER_HEADS, 2, PEER_KEYS, tt), lambda i: (0, 0, 0, i))],
        out_specs=[spec, spec, spec],
        out_shape=[shp, shp, shp],
        scratch_shapes=[pltpu.VMEM((2, 3 * SUBLANES, tt), F32), pltpu.VMEM((_NCAND, tt), F32)],
        compiler_params=_cparams(("parallel",)),
        name="peer_select",
    )(st)


def _peer_dense_kernel(x1_ref, xnt_ref, a_ref, b_ref, bd_ref, u_ref, vt_ref, fg_ref, o_ref,
                       acc_ref, *scr, rows, final_norm):
    nchunk = len(scr) // 2
    ht_refs, at_refs = scr[:nchunk], scr[nchunk:]
    j = pl.program_id(1)
    last = pl.num_programs(1) - 1

    @pl.when(j == 0)
    def _():
        acc_ref[...] = jnp.zeros_like(acc_ref)

    ch, tt = ht_refs[0].shape
    per = PEER_ROWS_PER_SLAB
    assert nchunk * ch == rows * PEER_KEYS
    lw = LANES
    assert SUBLANES % per == 0 and (ch // PEER_KEYS) % per == 0

    def proj(c):
        cs = slice(c * ch, (c + 1) * ch)
        ht_refs[c][...] = jnp.dot(u_ref[cs, :], xnt_ref[...], preferred_element_type=F32)

    def gate(c):
        for p in range(ch // PEER_KEYS // per):
            gate_rows(c, range(c * (ch // PEER_KEYS) + p * per, c * (ch // PEER_KEYS) + (p + 1) * per))

    def gate_rows(c, rr):
        heads = range(PEER_HEADS)
        for lt in range(tt // lw):
            ls = slice(lt * lw, (lt + 1) * lw)
            i0 = pl.multiple_of(j * rows + rr[0] // SUBLANES * SUBLANES, SUBLANES)
            sub = [r % SUBLANES for r in rr]
            att = [a_ref[h, pl.ds(i0, SUBLANES), ls] for h in heads]
            bdt = [bd_ref[h, pl.ds(i0, SUBLANES), ls] for h in heads]
            av = [[jnp.broadcast_to(att[h][q:q + 1], (SUBLANES, lw)) for h in heads] for q in sub]
            bdv = [[jnp.broadcast_to(bdt[h][q:q + 1], (SUBLANES, lw)) for h in heads] for q in sub]
            for sg in range(PEER_KEYS // (2 * SUBLANES)):
                halves = [[] for _ in rr]
                for half in range(2):
                    k0 = sg * 2 * SUBLANES + half * SUBLANES
                    bv = [b_ref[h, k0:k0 + SUBLANES, ls] for h in heads]
                    for ri, r in enumerate(rr):
                        t = [jnp.where(bv[h] >= bdv[ri][h], bv[h], 0.0) * av[ri][h] for h in heads]
                        while len(t) > 1:
                            t = [t[i] + t[i + 1] for i in range(0, len(t), 2)]
                        h0 = r * PEER_KEYS - c * ch + k0
                        ht = ht_refs[c][h0:h0 + SUBLANES, ls]
                        halves[ri].append(t[0] * (ht * (1.0 + lax.erf(ht * (2.0 ** -0.5)))))
                for ri, r in enumerate(rr):
                    o0 = r * PEER_KEYS - c * ch + sg * 2 * SUBLANES
                    at_refs[c][o0:o0 + 2 * SUBLANES, ls] = jnp.concatenate(halves[ri], axis=0).astype(BF16)

    def value(c):
        acc_ref[...] += jnp.dot(vt_ref[:, c * ch:(c + 1) * ch], at_refs[c][...], preferred_element_type=F32)

    proj(0)
    for c in range(nchunk):
        if c + 1 < nchunk:
            proj(c + 1)
        gate(c)
        value(c)

    @pl.when(j == last)
    def _():
        x2 = x1_ref[...] + acc_ref[...].T
        if final_norm:
            x2 = x2 * lax.rsqrt(jnp.mean(x2 * x2, axis=-1, keepdims=True) + EPS) * fg_ref[...]
        o_ref[...] = x2


def _peer_dense(x1, xn, a, b, bd, u_bf, vt_bf, fg, tt, eb, final_norm):
    T = x1.shape[0]
    rows = eb // PEER_KEYS
    ne = PEER_EXPERTS // eb
    hk = pl.BlockSpec((PEER_HEADS, PEER_KEYS, tt), lambda i, j: (0, 0, i))
    return pl.pallas_call(
        functools.partial(_peer_dense_kernel, rows=rows, final_norm=final_norm),
        grid=(T // tt, ne),
        in_specs=[pl.BlockSpec((tt, D_MODEL), lambda i, j: (i, 0)),
                  pl.BlockSpec((D_MODEL, tt), lambda i, j: (0, i)),
                  hk, hk, hk,
                  pl.BlockSpec((eb, D_MODEL), lambda i, j: (j, 0)),
                  pl.BlockSpec((D_MODEL, eb), lambda i, j: (0, j)),
                  pl.BlockSpec(fg.shape, lambda i, j: (0, 0))],
        out_specs=pl.BlockSpec((tt, D_MODEL), lambda i, j: (i, 0)),
        out_shape=jax.ShapeDtypeStruct((T, D_MODEL), F32),
        scratch_shapes=([pltpu.VMEM((D_MODEL, tt), F32)]
                        + [pltpu.VMEM((PEER_CHUNK_EXPERTS, tt), F32)] * (eb // PEER_CHUNK_EXPERTS)
                        + [pltpu.VMEM((PEER_CHUNK_EXPERTS, tt), BF16)] * (eb // PEER_CHUNK_EXPERTS)),
        compiler_params=_cparams(("parallel", "arbitrary")),
        name="peer_dense",
    )(x1, xn, a, b, bd, u_bf, vt_bf, fg)


def _pad_heads(w, axis=-1):
    w = jnp.moveaxis(w, axis, -1)
    lead = w.shape[:-1]
    w = w.reshape(lead + (ML_HEADS, ML_HEAD_DIM))
    w = jnp.pad(w, [(0, 0)] * len(lead) + [(0, 0), (0, LANES - ML_HEAD_DIM)])
    return jnp.moveaxis(w.reshape(lead + (ML_PAD,)), -1, axis)


def _layer_params(l, norm1_g, w_in, da_lambda, da_subln_g, ml_conv_w, ml_conv_b, ml_wq, ml_wk, ml_gate_b,
                  ml_norm_g, ml_skip, cm_norm_g, cm_ws, cm_b, w_out, norm2_g, peer_wq, peer_keys, peer_u, peer_v):
    o = np.cumsum((0, 512, 512, 512, 256, 256, 256, 4, 4, 256, 256))
    seg = lambda i: w_in[:, o[i]:o[i + 1]]
    gates = jnp.pad(jnp.concatenate([seg(6), seg(7)], axis=1), ((0, 0), (0, LANES - 2 * ML_HEADS)))
    w_pad = jnp.concatenate([seg(0), seg(1), seg(2), _pad_heads(seg(3)), _pad_heads(seg(4)), _pad_heads(seg(5)),
                             seg(8), seg(9), gates], axis=1).astype(BF16)
    pad_sq = lambda w: jnp.pad(w, ((0, 0), (0, LANES - ML_HEAD_DIM), (0, LANES - ML_HEAD_DIM)))
    return dict(
        lam_init=0.8 - 0.6 * math.exp(-0.3 * l),
        g1=norm1_g.reshape(1, D_MODEL), w_pad=w_pad,
        lam_p=da_lambda, subln_g=da_subln_g.reshape(1, DA_VDIM),
        cw=_pad_heads(ml_conv_w), cb=_pad_heads(ml_conv_b.reshape(1, ML_WIDTH)),
        wq=pad_sq(ml_wq), wk=pad_sq(ml_wk), gate_b=ml_gate_b,
        ng=_pad_heads(ml_norm_g.reshape(1, ML_WIDTH)), sk=_pad_heads(ml_skip.reshape(1, ML_WIDTH)),
        cmg=cm_norm_g.reshape(1, CM_WIDTH), ws=cm_ws,
        bmat=jnp.repeat(cm_b.T, CM_GROUP_DIM, axis=1),
        wda=w_out[:DA_WIDTH].astype(BF16),
        wml=_pad_heads(w_out[DA_WIDTH:DA_WIDTH + ML_WIDTH], axis=0).astype(BF16),
        wcm=w_out[DA_WIDTH + ML_WIDTH:].astype(BF16),
        g2=norm2_g.reshape(1, D_MODEL), pwq=peer_wq.astype(BF16), pkeys=peer_keys.astype(BF16),
        u_bf=peer_u.astype(BF16), vt_bf=peer_v.T.astype(BF16),
    )


def _pick(n, prefs):
    for p in prefs:
        if n % p == 0:
            return p
    return n


def _layer(x, past, p, bias, final_g, final_norm):
    B, L, _ = x.shape
    T = B * L
    x2d = x.reshape(T, D_MODEL)
    tm = _pick(T, (512, 256, 128))
    q, kf, vf, kb, vb, mc, mv, mo, cu, cv, gc = _in_proj(x2d, p["g1"], p["w_pad"], tm, past is None)
    r3 = lambda a: a.reshape(B, L, a.shape[-1])
    if past is None:
        g_t = jnp.broadcast_to(p["subln_g"].reshape(DA_VDIM, 1), (DA_VDIM, ATT_BLOCK))
        oda = _attn_prompt(q, r3(kb), vb, bias, p["lam_p"], g_t, p["lam_init"], B)
        c0 = jnp.zeros((B, ML_HEADS, LANES, LANES), F32)
        n0 = jnp.zeros((B, ML_HEADS, 1, LANES), F32)
        m0 = jnp.zeros((B, ML_HEADS, 1, LANES), F32)
        cv0 = jnp.zeros((B, ML_CONV - 1, ML_PAD), F32)
        lc = CM_CHUNK
    else:
        pk, pv, pc, pn, pm, pconv = past
        P = pk.shape[1]
        oda = _attn_step(r3(q), r3(kb), r3(vb), pk.reshape(B, P, DA_WIDTH), pv.reshape(B, P, DA_WIDTH),
                         bias[0], bias[1], p["lam_p"], p["subln_g"], p["lam_init"])
        c0 = jnp.pad(pc, ((0, 0), (0, 0), (0, LANES - ML_HEAD_DIM), (0, LANES - ML_HEAD_DIM)))
        n0 = jnp.pad(pn, ((0, 0), (0, 0), (0, LANES - ML_HEAD_DIM)))[:, :, None, :]
        m0 = jnp.broadcast_to(pm[:, :, None, None], (B, ML_HEADS, 1, LANES))
        cv0 = _pad_heads(pconv)
        lc = L
    oml, c_new, n_new, m_new, conv_new = _mlstm(
        p["gate_b"], r3(mc), r3(mv), r3(mo), r3(gc), p["cw"], p["cb"], p["wq"], p["wk"], p["ng"], p["sk"],
        c0, n0, m0, cv0, ML_SEQS_PER_STEP)
    x1, vcm = _mix_out(x2d, oda.reshape(T, DA_WIDTH), oml.reshape(T, ML_PAD), cu, cv, p["cmg"], p["ws"], p["bmat"],
                       p["wda"], p["wml"], p["wcm"], _pick(T, (256, 128)), lc)
    xn2, st = _peer_query(x1, p["g2"], p["pwq"], p["pkeys"], _pick(T, (256, 128)))
    a, b, bd = _peer_select(st, _pick(T, (128,)))
    x2 = _peer_dense(x1, xn2, a, b, bd, p["u_bf"], p["vt_bf"], final_g, _pick(T, (512, 256, 128)), PEER_BLOCK_EXPERTS, final_norm)
    state = (kf.reshape(B, L, DA_HEADS, 2 * DA_HEAD_DIM), vf.reshape(B, L, DA_HEADS, DA_VDIM),
             c_new[:, :, :ML_HEAD_DIM, :ML_HEAD_DIM], n_new[:, :, 0, :ML_HEAD_DIM], m_new[:, :, 0, 0],
             conv_new.reshape(B, ML_CONV - 1, ML_HEADS, LANES)[..., :ML_HEAD_DIM].reshape(B, ML_CONV - 1, ML_WIDTH),
             vcm.reshape(B, L, CM_WIDTH))
    return x2.reshape(B, L, D_MODEL), state


def kernel(x_prompt, x_sample, cache_k, cache_v, state_mlstm_c, state_mlstm_n, state_mlstm_m, state_mlstm_conv, norm1_g, w_in, da_lambda, da_subln_g, rel_bias_table, ml_conv_w, ml_conv_b, ml_wq, ml_wk, ml_gate_b, ml_norm_g, ml_skip, cm_norm_g, cm_ws, cm_b, w_out, norm2_g, peer_wq, peer_keys, peer_u, peer_v, final_g):
    depth = w_in.shape[0]
    S = x_prompt.shape[1]
    Ld = x_sample.shape[1]
    P = cache_k.shape[2]
    assert S % ATT_BLOCK == 0
    far_bucket = int(_rel_bucket_np(np.array(-(ATT_BLOCK + 1))))
    ii = np.arange(ATT_BLOCK)
    idx_prompt = np.stack([_rel_bucket_np(ii[:, None] - ii[None, :]),
                           _rel_bucket_np(ii[:, None] - ATT_BLOCK - ii[None, :])])
    bias_prompt = _bias_tiles(rel_bias_table, idx_prompt, far_bucket, LOG2E)
    qpos = P + np.arange(Ld)
    bias_past = _bias_tiles(rel_bias_table, _rel_bucket_np(np.arange(P)[None, :] - qpos[:, None])[None], None)
    bias_new = _bias_tiles(rel_bias_table, _rel_bucket_np(qpos[None, :] - qpos[:, None])[None], None)
    fg = final_g.reshape(1, D_MODEL)

    hp, hs = x_prompt, x_sample
    st_p, st_s = [], []
    for l in range(depth):
        p = _layer_params(l, norm1_g[l], w_in[l], da_lambda[l], da_subln_g[l], ml_conv_w[l], ml_conv_b[l],
                          ml_wq[l], ml_wk[l], ml_gate_b[l], ml_norm_g[l], ml_skip[l], cm_norm_g[l], cm_ws[l],
                          cm_b[l], w_out[l], norm2_g[l], peer_wq[l], peer_keys[l], peer_u[l], peer_v[l])
        last = l == depth - 1
        hp, sp = _layer(hp, None, p, bias_prompt, fg, last)
        past = (cache_k[l], cache_v[l], state_mlstm_c[l], state_mlstm_n[l], state_mlstm_m[l], state_mlstm_conv[l])
        hs, ss = _layer(hs, past, p, (bias_past, bias_new), fg, last)
        st_p.append(sp)
        st_s.append(ss)
    stack = lambda sts, i: jnp.stack([s[i] for s in sts])
    return (hp, hs,
            stack(st_p, 0), stack(st_p, 1), stack(st_p, 2), stack(st_p, 3), stack(st_p, 4), stack(st_p, 5),
            stack(st_s, 0), stack(st_s, 1), stack(st_s, 2), stack(st_s, 3), stack(st_s, 4), stack(st_s, 5),
            stack(st_s, 6))
```

```python
import functools
import math

import numpy as np
import jax
import jax.numpy as jnp
from jax import lax
from jax.experimental import pallas as pl
from jax.experimental.pallas import tpu as pltpu

F32 = jnp.float32
BF16 = jnp.bfloat16

D_MODEL = 1024
CHUNK = 64
DA_HEADS = 4
DA_HEAD_DIM = 64
DA_VDIM = 128
DA_WIDTH = 512
ML_HEADS = 4
ML_HEAD_DIM = 64
ML_WIDTH = 256
ML_CONV = 4
CM_GROUPS = 4
CM_WIDTH = 256
CM_GROUP_DIM = 64
CM_CHUNK = 128
PEER_HEADS = 8
PEER_KEYS = 128
PEER_EXPERTS = PEER_KEYS * PEER_KEYS
PEER_QDIM = 256
PEER_HALF = 128
PEER_TOPK = 16
REL_BUCKETS = 32
REL_MAX_DIST = 128
EPS = 1e-6
NEG_INF = -1e30

LANES = 128
SUBLANES = 8
MXU_N = 256
ATT_BLOCK = MXU_N
FAR_BLOCKS = 2
LOG2E = 1.4426950408889634
ML_SEQS_PER_STEP = 4
ML_PAD = ML_HEADS * LANES
VMEM_LIMIT = 56 * 1024 * 1024
PEER_BLOCK_EXPERTS = 2048
PEER_CHUNK_EXPERTS = 512
PEER_ROWS_PER_SLAB = 2

_C_Q, _C_K, _C_V = 0, 512, 1024
_C_MC, _C_MV, _C_MO = 1536, 2048, 2560
_C_CU, _C_CV, _C_G = 3072, 3328, 3584
IN_PAD = 3712


def _cparams(sem):
    return pltpu.CompilerParams(dimension_semantics=sem, vmem_limit_bytes=VMEM_LIMIT)


def _gelu(x):
    return 0.5 * x * (1.0 + lax.erf(x * (2.0 ** -0.5)))


def _dot_nt(a, b):
    return lax.dot_general(a, b, (((1,), (1,)), ((), ())), preferred_element_type=F32)


def _dot_tn(a, b):
    return lax.dot_general(a, b, (((0,), (0,)), ((), ())), preferred_element_type=F32)


def _inproj_kernel(x_ref, g_ref, w_ref, q_ref, kf_ref, vf_ref, kb_ref, vb_ref,
                   mc_ref, mv_ref, mo_ref, cu_ref, cv_ref, gc_ref, *, blocked_t):
    x = x_ref[...]
    xn = x * lax.rsqrt(jnp.mean(x * x, axis=-1, keepdims=True) + EPS) * g_ref[...]
    xb = xn.astype(BF16)

    def proj(lo, hi):
        return jnp.dot(xb, w_ref[:, lo:hi], preferred_element_type=F32)

    def put(ref, val, scale=None):
        if scale is not None:
            val = val * scale
        if blocked_t:
            for n in range(ref.shape[0]):
                ref[n] = val[n * ATT_BLOCK:(n + 1) * ATT_BLOCK].T.astype(BF16)
        else:
            ref[...] = val.astype(BF16)

    put(q_ref, proj(_C_Q, _C_K), DA_HEAD_DIM ** -0.5 * (LOG2E if blocked_t else 1.0))
    k = proj(_C_K, _C_V)
    kf_ref[...] = k
    kb_ref[...] = k.astype(BF16)
    v = proj(_C_V, _C_MC)
    vf_ref[...] = v
    put(vb_ref, v)
    mc_ref[...] = proj(_C_MC, _C_MV)
    mv_ref[...] = proj(_C_MV, _C_MO)
    mo_ref[...] = proj(_C_MO, _C_CU)
    cu_ref[...] = proj(_C_CU, _C_CV)
    cv_ref[...] = proj(_C_CV, _C_G)
    gc_ref[...] = proj(_C_G, IN_PAD)


def _in_proj(x2d, g, w_pad, tm, blocked_t):
    T = x2d.shape[0]
    row = lambda n: pl.BlockSpec((tm, n), lambda i: (i, 0))
    full = lambda a: pl.BlockSpec(a.shape, lambda i: (0,) * a.ndim)
    if blocked_t:
        tspec = pl.BlockSpec((tm // ATT_BLOCK, DA_WIDTH, ATT_BLOCK), lambda i: (i, 0, 0))
        tshape = jax.ShapeDtypeStruct((T // ATT_BLOCK, DA_WIDTH, ATT_BLOCK), BF16)
    else:
        tspec, tshape = row(DA_WIDTH), jax.ShapeDtypeStruct((T, DA_WIDTH), BF16)
    outs = [(DA_WIDTH, F32), (DA_WIDTH, F32), (DA_WIDTH, BF16), None,
            (ML_PAD, F32), (ML_PAD, F32), (ML_PAD, F32), (CM_WIDTH, F32), (CM_WIDTH, F32), (LANES, F32)]
    specs = [tspec] + [tspec if o is None else row(o[0]) for o in outs]
    shapes = [tshape] + [tshape if o is None else jax.ShapeDtypeStruct((T, o[0]), o[1]) for o in outs]
    return pl.pallas_call(
        functools.partial(_inproj_kernel, blocked_t=blocked_t),
        grid=(T // tm,),
        in_specs=[row(D_MODEL), full(g), full(w_pad)],
        out_specs=specs,
        out_shape=shapes,
        compiler_params=_cparams(("parallel",)),
        name="in_proj",
    )(x2d, g, w_pad)


def _rel_bucket_np(rel):
    half = REL_BUCKETS // 2
    max_exact = half // 2
    ret = np.where(rel > 0, half, 0)
    n = np.abs(rel)
    nf = np.maximum(n, 1).astype(np.float32)
    large = max_exact + (np.log(nf / np.float32(max_exact)) / np.float32(math.log(REL_MAX_DIST / max_exact))
                         * np.float32(half - max_exact)).astype(np.int32)
    large = np.minimum(large, half - 1)
    return (ret + np.where(n < max_exact, n, large)).astype(np.int32)


def _bias_kernel(tab_ref, idx_ref, o_ref, *, shift_bucket, scale):
    for n in range(idx_ref.shape[0]):
        idx = idx_ref[n]
        for h in range(DA_HEADS):
            acc = jnp.zeros(idx.shape, F32)
            for b in range(REL_BUCKETS):
                acc = jnp.where(idx == b, tab_ref[b, h], acc)
            if shift_bucket is not None:
                acc = acc - tab_ref[shift_bucket, h]
            o_ref[n, h] = acc * scale


def _bias_tiles(table, idx_np, shift_bucket, scale=1.0):
    n, r, c = idx_np.shape
    return pl.pallas_call(
        functools.partial(_bias_kernel, shift_bucket=shift_bucket, scale=scale),
        in_specs=[pl.BlockSpec(memory_space=pltpu.SMEM), pl.BlockSpec(memory_space=pltpu.VMEM)],
        out_specs=pl.BlockSpec(memory_space=pltpu.VMEM),
        out_shape=jax.ShapeDtypeStruct((n, DA_HEADS, r, c), F32),
        name="rel_bias",
    )(table, jnp.asarray(idx_np))


def _lambda(lam_ref, lam_init):
    lp = lam_ref[...]
    a = jnp.sum(lp[0:1] * lp[1:2], axis=-1, keepdims=True)
    b = jnp.sum(lp[2:3] * lp[3:4], axis=-1, keepdims=True)
    return jnp.exp(a) - jnp.exp(b) + lam_init


def _subln(o1, l1, o2, l2, lam, g, lam_init):
    o = o1 / l1 - lam * (o2 / l2)
    o = o * lax.rsqrt(jnp.mean(o * o, axis=-1, keepdims=True) + EPS) * g
    return o * (1.0 - lam_init)


def _attn_prompt_kernel(lam_ref, qt_ref, k_ref, vt_ref, bias_ref, g_ref, o_ref,
                        m_sc, l_sc, a_sc, acc_sc, s_sc, p_sc, *, lam_init):
    tb = ATT_BLOCK
    qb = pl.program_id(1)
    lam = _lambda(lam_ref, lam_init)
    krow = lax.broadcasted_iota(jnp.int32, (tb, tb), 0)
    qcol = lax.broadcasted_iota(jnp.int32, (tb, tb), 1)
    diag_mask = (krow // CHUNK) <= (qcol // CHUNK)
    comp0 = lax.broadcasted_iota(jnp.int32, (LANES, 1), 0) < DA_HEAD_DIM
    has_prev = qb >= 1

    m_sc[...] = jnp.full(m_sc.shape, NEG_INF, F32)
    l_sc[...] = jnp.zeros(l_sc.shape, F32)
    acc_sc[...] = jnp.zeros(acc_sc.shape, F32)

    qts = []
    for h in range(DA_HEADS):
        qt = qt_ref[0, LANES * h:LANES * (h + 1), :]
        qts.append((jnp.where(comp0, qt, jnp.zeros_like(qt)), jnp.where(comp0, jnp.zeros_like(qt), qt)))

    def sweep(kblk, bias_n=None, mask=None, nk=1):
        rk = slice(0, nk * tb)
        for h in range(DA_HEADS):
            kh = k_ref[0, pl.ds(pl.multiple_of(kblk * tb, tb), nk * tb), LANES * h:LANES * (h + 1)]
            for c in range(2):
                s = jnp.dot(kh, qts[h][c], preferred_element_type=F32)
                if bias_n is not None:
                    s = s + bias_ref[bias_n, h]
                if mask is not None:
                    s = jnp.where(mask, s, NEG_INF)
                s_sc[2 * h + c, rk] = s
        for ch in range(2 * DA_HEADS):
            s = s_sc[ch, rk]
            m_old = m_sc[ch]
            m_new = jnp.maximum(m_old, jnp.max(s, axis=0, keepdims=True))
            alpha = jnp.exp2(m_old - m_new)
            p = jnp.exp2(s - m_new[0:1])
            l_sc[ch] = alpha * l_sc[ch] + jnp.sum(p, axis=0, keepdims=True)
            m_sc[ch] = m_new
            a_sc[ch] = alpha
            p_sc[ch, rk] = p.astype(BF16)
        for h in range(DA_HEADS):
            vth = [vt_ref[kblk + n, LANES * h:LANES * (h + 1), :] for n in range(nk)]
            vth = vth[0] if nk == 1 else jnp.concatenate(vth, axis=1)
            for c in range(2):
                ch = 2 * h + c
                acc_sc[ch] = a_sc[ch][0:1] * acc_sc[ch] + jnp.dot(vth, p_sc[ch, rk], preferred_element_type=F32)

    sweep(qb, 0, diag_mask)
    sweep(jnp.maximum(qb - 1, 0), 1, has_prev)
    nfar = jnp.maximum(qb - 1, 0)
    lax.fori_loop(0, nfar // FAR_BLOCKS, lambda j, _: (sweep(FAR_BLOCKS * j, nk=FAR_BLOCKS), 0)[1], 0)
    lax.fori_loop(nfar // FAR_BLOCKS * FAR_BLOCKS, nfar, lambda j, _: (sweep(j), 0)[1], 0)

    for h in range(DA_HEADS):
        o = acc_sc[2 * h] / l_sc[2 * h][0:1] - lam * (acc_sc[2 * h + 1] / l_sc[2 * h + 1][0:1])
        o = o * lax.rsqrt(jnp.mean(o * o, axis=0, keepdims=True) + EPS) * g_ref[...]
        o_ref[0, :, LANES * h:LANES * (h + 1)] = (o * (1.0 - lam_init)).T.astype(o_ref.dtype)


def _attn_prompt(qt, k, vt, bias, lam_p, g_t, lam_init, B):
    S = k.shape[1]
    tb = ATT_BLOCK
    nb = S // tb
    return pl.pallas_call(
        functools.partial(_attn_prompt_kernel, lam_init=lam_init),
        grid=(B, nb),
        in_specs=[
            pl.BlockSpec(lam_p.shape, lambda b, i: (0, 0)),
            pl.BlockSpec((1, DA_WIDTH, tb), lambda b, i: (b * nb + i, 0, 0)),
            pl.BlockSpec((1, S, DA_WIDTH), lambda b, i: (b, 0, 0)),
            pl.BlockSpec((nb, DA_WIDTH, tb), lambda b, i: (b, 0, 0)),
            pl.BlockSpec(bias.shape, lambda b, i: (0, 0, 0, 0)),
            pl.BlockSpec(g_t.shape, lambda b, i: (0, 0)),
        ],
        out_specs=pl.BlockSpec((1, tb, DA_WIDTH), lambda b, i: (b, i, 0)),
        out_shape=jax.ShapeDtypeStruct((B, S, DA_WIDTH), BF16),
        scratch_shapes=[pltpu.VMEM((2 * DA_HEADS, SUBLANES, tb), F32),
                        pltpu.VMEM((2 * DA_HEADS, SUBLANES, tb), F32),
                        pltpu.VMEM((2 * DA_HEADS, SUBLANES, tb), F32),
                        pltpu.VMEM((2 * DA_HEADS, LANES, tb), F32),
                        pltpu.VMEM((2 * DA_HEADS, FAR_BLOCKS * tb, tb), F32),
                        pltpu.VMEM((2 * DA_HEADS, FAR_BLOCKS * tb, tb), BF16)],
        compiler_params=_cparams(("parallel", "arbitrary")),
        name="attn_prompt",
    )(lam_p, qt, k, vt, bias, g_t)


def _attn_step_kernel(lam_ref, q_ref, kn_ref, vn_ref, kp_ref, vp_ref, bp_ref, bn_ref, g_ref, o_ref, *, lam_init):
    lam = _lambda(lam_ref, lam_init)
    first = lax.broadcasted_iota(jnp.int32, (1, LANES), 1) < DA_HEAD_DIM
    g = g_ref[...]
    for h in range(DA_HEADS):
        sl = slice(LANES * h, LANES * (h + 1))
        q = q_ref[0, :, sl]
        kn = kn_ref[0, :, sl]
        vn = vn_ref[0, :, sl]
        kp = kp_ref[0, :, sl].astype(BF16)
        vp = vp_ref[0, :, sl].astype(BF16)
        res = []
        for c in range(2):
            qc = jnp.where(first if c == 0 else jnp.logical_not(first), q, jnp.zeros_like(q))
            sp = _dot_nt(qc, kp) + bp_ref[0, h]
            sn = _dot_nt(qc, kn) + bn_ref[0, h]
            m = jnp.maximum(jnp.max(sp, axis=-1, keepdims=True), jnp.max(sn, axis=-1, keepdims=True))
            pp = jnp.exp(sp - m)
            pn = jnp.exp(sn - m)
            l = jnp.sum(pp, axis=-1, keepdims=True) + jnp.sum(pn, axis=-1, keepdims=True)
            acc = (jnp.dot(pp.astype(BF16), vp, preferred_element_type=F32)
                   + jnp.dot(pn.astype(BF16), vn, preferred_element_type=F32))
            res.append((l, acc))
        (l1, a1), (l2, a2) = res
        o_ref[0, :, sl] = _subln(a1, l1, a2, l2, lam, g, lam_init).astype(o_ref.dtype)


def _attn_step(q, kn, vn, kp, vp, bias_p, bias_n, lam_p, g, lam_init):
    B, L, _ = q.shape
    P = kp.shape[1]
    c3 = lambda b: (b, 0, 0)
    z4 = lambda b: (0, 0, 0, 0)
    return pl.pallas_call(
        functools.partial(_attn_step_kernel, lam_init=lam_init),
        grid=(B,),
        in_specs=[
            pl.BlockSpec(lam_p.shape, lambda b: (0, 0)),
            pl.BlockSpec((1, L, DA_WIDTH), c3), pl.BlockSpec((1, L, DA_WIDTH), c3), pl.BlockSpec((1, L, DA_WIDTH), c3),
            pl.BlockSpec((1, P, DA_WIDTH), c3), pl.BlockSpec((1, P, DA_WIDTH), c3),
            pl.BlockSpec(bias_p.shape, z4), pl.BlockSpec(bias_n.shape, z4),
            pl.BlockSpec(g.shape, lambda b: (0, 0)),
        ],
        out_specs=pl.BlockSpec((1, L, DA_WIDTH), c3),
        out_shape=jax.ShapeDtypeStruct((B, L, DA_WIDTH), BF16),
        compiler_params=_cparams(("parallel",)),
        name="attn_step",
    )(lam_p, q, kn, vn, kp, vp, bias_p, bias_n, g)


def _log_sigmoid(x):
    return jnp.minimum(x, 0.0) - jnp.log(1.0 + jnp.exp(-jnp.abs(x)))


def _mlstm_kernel(gb_ref, mc_ref, mv_ref, mo_ref, gc_ref, cw_ref, cb_ref, wq_ref, wk_ref, ng_ref, sk_ref,
                  c0_ref, n0_ref, m0_ref, cv0_ref,
                  o_ref, co_ref, no_ref, mo2_ref, cvo_ref,
                  xbuf, c_sc, n_sc, m_sc):
    j = pl.program_id(1)
    L = CHUNK
    bb = mc_ref.shape[0]
    chains = [(b, h) for b in range(bb) for h in range(ML_HEADS)]
    hsl = lambda h: slice(LANES * h, LANES * (h + 1))

    @pl.when(j == 0)
    def _():
        xbuf[:, 5:8, :] = cv0_ref[...]
        c_sc[...] = c0_ref[...]
        n_sc[...] = n0_ref[...]
        m_sc[...] = m0_ref[...]

    r = lax.broadcasted_iota(jnp.int32, (L, L), 0)
    s = lax.broadcasted_iota(jnp.int32, (L, L), 1)
    tril = s <= r

    cc, gcol, gt = [], [], []
    for b in range(bb):
        x = mc_ref[b]
        xbuf[b, 8:8 + L, :] = x
        y = (cb_ref[...] + cw_ref[3:4, :] * x + cw_ref[2:3, :] * xbuf[b, 7:7 + L, :]
             + cw_ref[1:2, :] * xbuf[b, 6:6 + L, :] + cw_ref[0:1, :] * xbuf[b, 5:5 + L, :])
        cc.append(y * jax.nn.sigmoid(y))
        tail = xbuf[b, 5 + L:8 + L, :]
        xbuf[b, 5:8, :] = tail
        cvo_ref[b] = tail
        g = gc_ref[b]
        gcol.append(g)
        gt.append(jnp.concatenate([g, jnp.zeros((LANES - L, LANES), F32)], axis=0).T)

    qs, ks = {}, {}
    for b, h in chains:
        cch = cc[b][:, hsl(h)]
        qs[b, h] = jnp.dot(cch, wq_ref[h], preferred_element_type=F32)
        ks[b, h] = jnp.dot(cch, wk_ref[h], preferred_element_type=F32) * (ML_HEAD_DIM ** -0.5)
    sc, rd = {}, {}
    for b, h in chains:
        sc[b, h] = _dot_nt(qs[b, h], ks[b, h])
        rd[b, h] = jnp.dot(qs[b, h], c_sc[b, h], preferred_element_type=F32)
    ig_c, ig_r, lf_c, lf_r = {}, {}, {}, {}
    for b, h in chains:
        b_i = gb_ref[0, h]
        b_f = gb_ref[1, h]
        ig_c[b, h] = gcol[b][:, h:h + 1] + b_i
        lf_c[b, h] = _log_sigmoid(gcol[b][:, 4 + h:5 + h] + b_f)
        ig_r[b, h] = gt[b][h:h + 1, 0:L] + b_i
        lf_r[b, h] = _log_sigmoid(gt[b][4 + h:5 + h, 0:L] + b_f)
    f_c, f_r = {}, {}
    for ch in chains:
        f_c[ch] = jnp.sum(jnp.where(tril, lf_r[ch], 0.0), axis=1, keepdims=True)
        f_r[ch] = jnp.sum(jnp.where(r <= s, lf_c[ch], 0.0), axis=0, keepdims=True)
    dmat, dmax, tmax, m_prev, fl = {}, {}, {}, {}, {}
    for b, h in chains:
        ch = (b, h)
        m_prev[ch] = m_sc[b, h][0:1, 0:1]
        fl[ch] = f_c[ch][L - 1:L, :]
        dmat[ch] = jnp.where(tril, f_c[ch] - f_r[ch] + ig_r[ch], NEG_INF)
        dmax[ch] = jnp.max(dmat[ch], axis=1, keepdims=True)
        tmax[ch] = jnp.max(fl[ch] - f_r[ch] + ig_r[ch], axis=1, keepdims=True)
    sw, iw, m_t, kw, wc, m_new = {}, {}, {}, {}, {}, {}
    for ch in chains:
        inter = f_c[ch] + m_prev[ch]
        m_t[ch] = jnp.maximum(inter, dmax[ch])
        sw[ch] = sc[ch] * jnp.exp(dmat[ch] - m_t[ch])
        iw[ch] = jnp.exp(inter - m_t[ch])
        m_new[ch] = jnp.maximum(fl[ch] + m_prev[ch], tmax[ch])
        wc[ch] = jnp.exp(fl[ch] + m_prev[ch] - m_new[ch])
        kw[ch] = ks[ch] * jnp.exp(fl[ch] - f_c[ch] + ig_c[ch] - m_new[ch])
    nv, up = {}, {}
    for b, h in chains:
        vh = mv_ref[b, :, hsl(h)]
        nv[b, h] = jnp.dot(sw[b, h], vh, preferred_element_type=F32)
        up[b, h] = _dot_tn(kw[b, h], vh)
    n_prev, den, ksum = {}, {}, {}
    for b, h in chains:
        ch = (b, h)
        n_prev[ch] = n_sc[b, h]
        den[ch] = (jnp.sum(sw[ch], axis=1, keepdims=True)
                   + iw[ch] * jnp.sum(qs[ch] * n_prev[ch], axis=1, keepdims=True))
        ksum[ch] = jnp.sum(kw[ch], axis=0, keepdims=True)
    hout, ssq = {}, {}
    for ch in chains:
        hout[ch] = (nv[ch] + iw[ch] * rd[ch]) / jnp.maximum(jnp.abs(den[ch]), jnp.exp(-m_t[ch]))
        ssq[ch] = jnp.sum(hout[ch] * hout[ch], axis=1, keepdims=True)
    for b, h in chains:
        ch = (b, h)
        c_new = wc[ch] * c_sc[b, h] + up[ch]
        n_new = wc[ch] * n_prev[ch] + ksum[ch]
        m_row = jnp.broadcast_to(m_new[ch], (1, LANES))
        c_sc[b, h] = c_new
        n_sc[b, h] = n_new
        m_sc[b, h] = m_row
        co_ref[b, h] = c_new
        no_ref[b, h] = n_new
        mo2_ref[b, h] = m_row
        hn = hout[ch] * lax.rsqrt(ssq[ch] * (1.0 / ML_HEAD_DIM) + EPS) * ng_ref[:, hsl(h)]
        o_ref[b, :, hsl(h)] = ((hn + sk_ref[:, hsl(h)] * cc[b][:, hsl(h)])
                               * jax.nn.sigmoid(mo_ref[b, :, hsl(h)])).astype(o_ref.dtype)


def _mlstm(gate_b, mc, mv, mo, gc, cw, cb, wq, wk, ng, sk, c0, n0, m0, cv0, bb):
    B, L, _ = mc.shape
    nc = L // CHUNK
    tok = lambda n: pl.BlockSpec((bb, CHUNK, n), lambda b, j: (b, j, 0))
    full = lambda a: pl.BlockSpec(a.shape, lambda b, j: (0,) * a.ndim)
    st4 = lambda a: pl.BlockSpec((bb,) + a.shape[1:], lambda b, j: (b,) + (0,) * (a.ndim - 1))
    return pl.pallas_call(
        _mlstm_kernel,
        grid=(B // bb, nc),
        in_specs=[pl.BlockSpec(memory_space=pltpu.SMEM),
                  tok(ML_PAD), tok(ML_PAD), tok(ML_PAD), tok(LANES),
                  full(cw), full(cb), full(wq), full(wk), full(ng), full(sk),
                  st4(c0), st4(n0), st4(m0), st4(cv0)],
        out_specs=[tok(ML_PAD), st4(c0), st4(n0), st4(m0), st4(cv0)],
        out_shape=[jax.ShapeDtypeStruct((B, L, ML_PAD), BF16),
                   jax.ShapeDtypeStruct(c0.shape, F32), jax.ShapeDtypeStruct(n0.shape, F32),
                   jax.ShapeDtypeStruct(m0.shape, F32), jax.ShapeDtypeStruct(cv0.shape, F32)],
        scratch_shapes=[pltpu.VMEM((bb, 8 + CHUNK, ML_PAD), F32),
                        pltpu.VMEM((bb, ML_HEADS, LANES, LANES), F32),
                        pltpu.VMEM((bb, ML_HEADS, 1, LANES), F32),
                        pltpu.VMEM((bb, ML_HEADS, 1, LANES), F32)],
        compiler_params=_cparams(("parallel", "arbitrary")),
        name="mlstm",
    )(gate_b, mc, mv, mo, gc, cw, cb, wq, wk, ng, sk, c0, n0, m0, cv0)


def _mixout_kernel(x_ref, oda_ref, oml_ref, cu_ref, cv_ref, cmg_ref, ws_ref, bm_ref,
                   wda_ref, wml_ref, wcm_ref, x1_ref, vcm_ref, *, lc):
    tm = x_ref.shape[0]
    u = _gelu(cu_ref[...])
    gv = _gelu(cv_ref[...])
    vcm = gv * lax.rsqrt(jnp.mean(gv * gv, axis=-1, keepdims=True) + EPS) * cmg_ref[...]
    vcm_ref[...] = vcm
    r = lax.broadcasted_iota(jnp.int32, (lc, lc), 0)
    c = lax.broadcasted_iota(jnp.int32, (lc, lc), 1)
    grp = lax.broadcasted_iota(jnp.int32, (1, CM_WIDTH), 1) // CM_GROUP_DIM
    wsm = [jnp.where(c <= r, ws_ref[g, 0:lc, 0:lc], 0.0).astype(BF16) for g in range(CM_GROUPS)]
    pieces = []
    for ci in range(tm // lc):
        vch = vcm[ci * lc:(ci + 1) * lc].astype(BF16)
        mixed = bm_ref[0:lc, :]
        for g in range(CM_GROUPS):
            mixed = mixed + jnp.where(grp == g, jnp.dot(wsm[g], vch, preferred_element_type=F32), 0.0)
        pieces.append(u[ci * lc:(ci + 1) * lc] * mixed)
    ocm = jnp.concatenate(pieces, axis=0) if len(pieces) > 1 else pieces[0]
    y = (jnp.dot(oda_ref[...], wda_ref[...], preferred_element_type=F32)
         + jnp.dot(oml_ref[...], wml_ref[...], preferred_element_type=F32)
         + jnp.dot(ocm.astype(BF16), wcm_ref[...], preferred_element_type=F32))
    x1_ref[...] = x_ref[...] + y


def _mix_out(x2d, oda, oml, cu, cv, cmg, ws, bmat, wda, wml, wcm, tm, lc):
    T = x2d.shape[0]
    row = lambda n: pl.BlockSpec((tm, n), lambda i: (i, 0))
    full = lambda a: pl.BlockSpec(a.shape, lambda i: (0,) * a.ndim)
    return pl.pallas_call(
        functools.partial(_mixout_kernel, lc=lc),
        grid=(T // tm,),
        in_specs=[row(D_MODEL), row(DA_WIDTH), row(ML_PAD), row(CM_WIDTH), row(CM_WIDTH),
                  full(cmg), full(ws), full(bmat), full(wda), full(wml), full(wcm)],
        out_specs=[row(D_MODEL), row(CM_WIDTH)],
        out_shape=[jax.ShapeDtypeStruct((T, D_MODEL), F32), jax.ShapeDtypeStruct((T, CM_WIDTH), F32)],
        compiler_params=_cparams(("parallel",)),
        name="mix_out",
    )(x2d, oda, oml, cu, cv, cmg, ws, bmat, wda, wml, wcm)


_NTOP = PEER_TOPK + 1
_PAIRS = [(a, b) for a in range(_NTOP) for b in range(_NTOP) if (a + 1) * (b + 1) <= _NTOP]
_NCAND = ((len(_PAIRS) + 7) // 8) * 8


def _merge_exchange_network(n):
    pairs = []
    t = max(1, math.ceil(math.log2(n)))
    p = 1 << (t - 1)
    while p > 0:
        q, r, d = 1 << (t - 1), 0, p
        while d > 0:
            pairs += [(i, i + d) for i in range(n - d) if (i & p) == r]
            d, q, r = q - p, q >> 1, p
        p >>= 1
    return pairs


_KEY_SLABS = PEER_KEYS // SUBLANES
_SORT_NET = _merge_exchange_network(_KEY_SLABS)


def _peer_select_lanes(st_ref, a_ref, b_ref, bd_ref, vals, cand, ls):
    tt = ls.stop - ls.start
    for h in range(PEER_HEADS):
        for c in range(2):
            v = [st_ref[h, c, SUBLANES * j:SUBLANES * (j + 1), ls] for j in range(_KEY_SLABS)]
            for i, j in _SORT_NET:
                v[i], v[j] = jnp.maximum(v[i], v[j]), jnp.minimum(v[i], v[j])
            for i in range(_NTOP):
                m = jnp.max(v[0], axis=0, keepdims=True)
                vals[c, i:i + 1, :] = m
                keep = _NTOP - 1 - i
                popped = v[0] == m
                for j in range(min(keep, _KEY_SLABS)):
                    v[j] = jnp.where(popped, v[j + 1] if j + 1 < _KEY_SLABS else NEG_INF, v[j])
        cand[...] = jnp.full((_NCAND, tt), NEG_INF, F32)
        for i, (a, b) in enumerate(_PAIRS):
            cand[i:i + 1, :] = vals[0, a:a + 1, :] + vals[1, b:b + 1, :]
        top1 = vals[0, 0:1, :]
        top2 = vals[1, 0:1, :]
        best = top1 + top2
        cur = cand[...]
        z = jnp.zeros((1, tt), F32)
        c16 = best
        for i in range(PEER_TOPK):
            c16 = jnp.max(cur, axis=0, keepdims=True)
            z = z + jnp.exp(c16 - best)
            cur = jnp.where(cur == c16, NEG_INF, cur)
        thr = 0.5 * (c16 + jnp.max(cur, axis=0, keepdims=True))
        s1 = st_ref[h, 0, :, ls]
        a_ref[h, :, ls] = 0.5 * jnp.exp(s1 - top1) / z
        b_ref[h, :, ls] = jnp.exp(st_ref[h, 1, :, ls] - top2)
        bd_ref[h, :, ls] = jnp.exp(thr - s1 - top2)


def _peer_route_kernel(x_ref, g_ref, wq_ref, keys_ref, xnt_ref, a_ref, b_ref, bd_ref, st_sc, vals, cand):
    x = x_ref[...]
    xf = x * lax.rsqrt(jnp.mean(x * x, axis=-1, keepdims=True) + EPS) * g_ref[...]
    xn = xf.astype(BF16)
    xnt_ref[...] = xf.T.astype(BF16)
    qs = [jnp.dot(xn, wq_ref[:, PEER_QDIM * h:PEER_QDIM * (h + 1)], preferred_element_type=F32)
          for h in range(PEER_HEADS)]
    qs = [(q * lax.rsqrt(jnp.mean(q * q, axis=-1, keepdims=True) + EPS)).astype(BF16) for q in qs]
    for h in range(PEER_HEADS):
        for c in range(2):
            qc = qs[h][:, PEER_HALF * c:PEER_HALF * (c + 1)]
            st_sc[h, c] = _dot_nt(keys_ref[h, c], qc)
    for lt in range(x_ref.shape[0] // LANES):
        _peer_select_lanes(st_sc, a_ref, b_ref, bd_ref, vals, cand, slice(lt * LANES, (lt + 1) * LANES))


def _peer_route(x1, g, wq, keys, tm):
    T = x1.shape[0]
    full = lambda a: pl.BlockSpec(a.shape, lambda i: (0,) * a.ndim)
    spec = pl.BlockSpec((PEER_HEADS, PEER_KEYS, tm), lambda i: (0, 0, i))
    shp = jax.ShapeDtypeStruct((PEER_HEADS, PEER_KEYS, T), F32)
    return pl.pallas_call(
        _peer_route_kernel,
        grid=(T // tm,),
        in_specs=[pl.BlockSpec((tm, D_MODEL), lambda i: (i, 0)), full(g), full(wq), full(keys)],
        out_specs=[pl.BlockSpec((D_MODEL, tm), lambda i: (0, i)), spec, spec, spec],
        out_shape=[jax.ShapeDtypeStruct((D_MODEL, T), BF16), shp, shp, shp],
        scratch_shapes=[pltpu.VMEM((PEER_HEADS, 2, PEER_KEYS, tm), F32),
                        pltpu.VMEM((2, 3 * SUBLANES, LANES), F32), pltpu.VMEM((_NCAND, LANES), F32)],
        compiler_params=_cparams(("parallel",)),
        name="peer_route",
    )(x1, g, wq, keys)


def _peer_dense_kernel(x1_ref, xnt_ref, a_ref, b_ref, bd_ref, u_ref, vt_ref, fg_ref, o_ref,
                       acc_ref, *scr, rows, final_norm):
    nchunk = len(scr) // 2
    ht_refs, at_refs = scr[:nchunk], scr[nchunk:]
    j = pl.program_id(1)
    last = pl.num_programs(1) - 1

    @pl.when(j == 0)
    def _():
        acc_ref[...] = jnp.zeros_like(acc_ref)

    ch, tt = ht_refs[0].shape
    per = PEER_ROWS_PER_SLAB
    assert nchunk * ch == rows * PEER_KEYS
    lw = LANES
    assert SUBLANES % per == 0 and (ch // PEER_KEYS) % per == 0

    def proj(c):
        cs = slice(c * ch, (c + 1) * ch)
        ht_refs[c][...] = jnp.dot(u_ref[cs, :], xnt_ref[...], preferred_element_type=F32)

    def gate(c):
        for p in range(ch // PEER_KEYS // per):
            gate_rows(c, range(c * (ch // PEER_KEYS) + p * per, c * (ch // PEER_KEYS) + (p + 1) * per))

    def gate_rows(c, rr):
        heads = range(PEER_HEADS)
        for lt in range(tt // lw):
            ls = slice(lt * lw, (lt + 1) * lw)
            i0 = pl.multiple_of(j * rows + rr[0] // SUBLANES * SUBLANES, SUBLANES)
            sub = [r % SUBLANES for r in rr]
            att = [a_ref[h, pl.ds(i0, SUBLANES), ls] for h in heads]
            bdt = [bd_ref[h, pl.ds(i0, SUBLANES), ls] for h in heads]
            av = [[jnp.broadcast_to(att[h][q:q + 1], (SUBLANES, lw)) for h in heads] for q in sub]
            bdv = [[jnp.broadcast_to(bdt[h][q:q + 1], (SUBLANES, lw)) for h in heads] for q in sub]
            for sg in range(PEER_KEYS // (2 * SUBLANES)):
                halves = [[] for _ in rr]
                for half in range(2):
                    k0 = sg * 2 * SUBLANES + half * SUBLANES
                    bv = [b_ref[h, k0:k0 + SUBLANES, ls] for h in heads]
                    for ri, r in enumerate(rr):
                        t = [jnp.where(bv[h] >= bdv[ri][h], bv[h], 0.0) * av[ri][h] for h in heads]
                        while len(t) > 1:
                            t = [t[i] + t[i + 1] for i in range(0, len(t), 2)]
                        h0 = r * PEER_KEYS - c * ch + k0
                        ht = ht_refs[c][h0:h0 + SUBLANES, ls]
                        halves[ri].append(t[0] * (ht * (1.0 + lax.erf(ht * (2.0 ** -0.5)))))
                for ri, r in enumerate(rr):
                    o0 = r * PEER_KEYS - c * ch + sg * 2 * SUBLANES
                    at_refs[c][o0:o0 + 2 * SUBLANES, ls] = jnp.concatenate(halves[ri], axis=0).astype(BF16)

    def value(c):
        acc_ref[...] += jnp.dot(vt_ref[:, c * ch:(c + 1) * ch], at_refs[c][...], preferred_element_type=F32)

    proj(0)
    for c in range(nchunk):
        if c + 1 < nchunk:
            proj(c + 1)
        gate(c)
        value(c)

    @pl.when(j == last)
    def _():
        x2 = x1_ref[...] + acc_ref[...].T
        if final_norm:
            x2 = x2 * lax.rsqrt(jnp.mean(x2 * x2, axis=-1, keepdims=True) + EPS) * fg_ref[...]
        o_ref[...] = x2


def _peer_dense(x1, xn, a, b, bd, u_bf, vt_bf, fg, tt, eb, final_norm):
    T = x1.shape[0]
    rows = eb // PEER_KEYS
    ne = PEER_EXPERTS // eb
    hk = pl.BlockSpec((PEER_HEADS, PEER_KEYS, tt), lambda i, j: (0, 0, i))
    return pl.pallas_call(
        functools.partial(_peer_dense_kernel, rows=rows, final_norm=final_norm),
        grid=(T // tt, ne),
        in_specs=[pl.BlockSpec((tt, D_MODEL), lambda i, j: (i, 0)),
                  pl.BlockSpec((D_MODEL, tt), lambda i, j: (0, i)),
                  hk, hk, hk,
                  pl.BlockSpec((eb, D_MODEL), lambda i, j: (j, 0)),
                  pl.BlockSpec((D_MODEL, eb), lambda i, j: (0, j)),
                  pl.BlockSpec(fg.shape, lambda i, j: (0, 0))],
        out_specs=pl.BlockSpec((tt, D_MODEL), lambda i, j: (i, 0)),
        out_shape=jax.ShapeDtypeStruct((T, D_MODEL), F32),
        scratch_shapes=([pltpu.VMEM((D_MODEL, tt), F32)]
                        + [pltpu.VMEM((PEER_CHUNK_EXPERTS, tt), F32)] * (eb // PEER_CHUNK_EXPERTS)
                        + [pltpu.VMEM((PEER_CHUNK_EXPERTS, tt), BF16)] * (eb // PEER_CHUNK_EXPERTS)),
        compiler_params=_cparams(("parallel", "arbitrary")),
        name="peer_dense",
    )(x1, xn, a, b, bd, u_bf, vt_bf, fg)


def _pad_heads(w, axis=-1):
    w = jnp.moveaxis(w, axis, -1)
    lead = w.shape[:-1]
    w = w.reshape(lead + (ML_HEADS, ML_HEAD_DIM))
    w = jnp.pad(w, [(0, 0)] * len(lead) + [(0, 0), (0, LANES - ML_HEAD_DIM)])
    return jnp.moveaxis(w.reshape(lead + (ML_PAD,)), -1, axis)


def _layer_params(l, norm1_g, w_in, da_lambda, da_subln_g, ml_conv_w, ml_conv_b, ml_wq, ml_wk, ml_gate_b,
                  ml_norm_g, ml_skip, cm_norm_g, cm_ws, cm_b, w_out, norm2_g, peer_wq, peer_keys, peer_u, peer_v):
    o = np.cumsum((0, 512, 512, 512, 256, 256, 256, 4, 4, 256, 256))
    seg = lambda i: w_in[:, o[i]:o[i + 1]]
    gates = jnp.pad(jnp.concatenate([seg(6), seg(7)], axis=1), ((0, 0), (0, LANES - 2 * ML_HEADS)))
    w_pad = jnp.concatenate([seg(0), seg(1), seg(2), _pad_heads(seg(3)), _pad_heads(seg(4)), _pad_heads(seg(5)),
                             seg(8), seg(9), gates], axis=1).astype(BF16)
    pad_sq = lambda w: jnp.pad(w, ((0, 0), (0, LANES - ML_HEAD_DIM), (0, LANES - ML_HEAD_DIM)))
    return dict(
        lam_init=0.8 - 0.6 * math.exp(-0.3 * l),
        g1=norm1_g.reshape(1, D_MODEL), w_pad=w_pad,
        lam_p=da_lambda, subln_g=da_subln_g.reshape(1, DA_VDIM),
        cw=_pad_heads(ml_conv_w), cb=_pad_heads(ml_conv_b.reshape(1, ML_WIDTH)),
        wq=pad_sq(ml_wq), wk=pad_sq(ml_wk), gate_b=ml_gate_b,
        ng=_pad_heads(ml_norm_g.reshape(1, ML_WIDTH)), sk=_pad_heads(ml_skip.reshape(1, ML_WIDTH)),
        cmg=cm_norm_g.reshape(1, CM_WIDTH), ws=cm_ws,
        bmat=jnp.repeat(cm_b.T, CM_GROUP_DIM, axis=1),
        wda=w_out[:DA_WIDTH].astype(BF16),
        wml=_pad_heads(w_out[DA_WIDTH:DA_WIDTH + ML_WIDTH], axis=0).astype(BF16),
        wcm=w_out[DA_WIDTH + ML_WIDTH:].astype(BF16),
        g2=norm2_g.reshape(1, D_MODEL), pwq=peer_wq.astype(BF16), pkeys=peer_keys.astype(BF16),
        u_bf=peer_u.astype(BF16), vt_bf=peer_v.T.astype(BF16),
    )


def _pick(n, prefs):
    for p in prefs:
        if n % p == 0:
            return p
    return n


def _layer(x, past, p, bias, final_g, final_norm):
    B, L, _ = x.shape
    T = B * L
    x2d = x.reshape(T, D_MODEL)
    tm = _pick(T, (512, 256, 128))
    q, kf, vf, kb, vb, mc, mv, mo, cu, cv, gc = _in_proj(x2d, p["g1"], p["w_pad"], tm, past is None)
    r3 = lambda a: a.reshape(B, L, a.shape[-1])
    if past is None:
        g_t = jnp.broadcast_to(p["subln_g"].reshape(DA_VDIM, 1), (DA_VDIM, ATT_BLOCK))
        oda = _attn_prompt(q, r3(kb), vb, bias, p["lam_p"], g_t, p["lam_init"], B)
        c0 = jnp.zeros((B, ML_HEADS, LANES, LANES), F32)
        n0 = jnp.zeros((B, ML_HEADS, 1, LANES), F32)
        m0 = jnp.zeros((B, ML_HEADS, 1, LANES), F32)
        cv0 = jnp.zeros((B, ML_CONV - 1, ML_PAD), F32)
        lc = CM_CHUNK
    else:
        pk, pv, pc, pn, pm, pconv = past
        P = pk.shape[1]
        oda = _attn_step(r3(q), r3(kb), r3(vb), pk.reshape(B, P, DA_WIDTH), pv.reshape(B, P, DA_WIDTH),
                         bias[0], bias[1], p["lam_p"], p["subln_g"], p["lam_init"])
        c0 = jnp.pad(pc, ((0, 0), (0, 0), (0, LANES - ML_HEAD_DIM), (0, LANES - ML_HEAD_DIM)))
        n0 = jnp.pad(pn, ((0, 0), (0, 0), (0, LANES - ML_HEAD_DIM)))[:, :, None, :]
        m0 = jnp.broadcast_to(pm[:, :, None, None], (B, ML_HEADS, 1, LANES))
        cv0 = _pad_heads(pconv)
        lc = L
    oml, c_new, n_new, m_new, conv_new = _mlstm(
        p["gate_b"], r3(mc), r3(mv), r3(mo), r3(gc), p["cw"], p["cb"], p["wq"], p["wk"], p["ng"], p["sk"],
        c0, n0, m0, cv0, ML_SEQS_PER_STEP)
    x1, vcm = _mix_out(x2d, oda.reshape(T, DA_WIDTH), oml.reshape(T, ML_PAD), cu, cv, p["cmg"], p["ws"], p["bmat"],
                       p["wda"], p["wml"], p["wcm"], _pick(T, (256, 128)), lc)
    xn2, a, b, bd = _peer_route(x1, p["g2"], p["pwq"], p["pkeys"], _pick(T, (256, 128)))
    x2 = _peer_dense(x1, xn2, a, b, bd, p["u_bf"], p["vt_bf"], final_g, _pick(T, (512, 256, 128)), PEER_BLOCK_EXPERTS, final_norm)
    state = (kf, vf,

             c_new[:, :, :ML_HEAD_DIM, :ML_HEAD_DIM], n_new[:, :, 0, :ML_HEAD_DIM], m_new[:, :, 0, 0],
             conv_new.reshape(B, ML_CONV - 1, ML_HEADS, LANES)[..., :ML_HEAD_DIM].reshape(B, ML_CONV - 1, ML_WIDTH),
             vcm.reshape(B, L, CM_WIDTH))
    return x2.reshape(B, L, D_MODEL), state


def kernel(x_prompt, x_sample, cache_k, cache_v, state_mlstm_c, state_mlstm_n, state_mlstm_m, state_mlstm_conv, norm1_g, w_in, da_lambda, da_subln_g, rel_bias_table, ml_conv_w, ml_conv_b, ml_wq, ml_wk, ml_gate_b, ml_norm_g, ml_skip, cm_norm_g, cm_ws, cm_b, w_out, norm2_g, peer_wq, peer_keys, peer_u, peer_v, final_g):
    depth = w_in.shape[0]
    S = x_prompt.shape[1]
    Ld = x_sample.shape[1]
    P = cache_k.shape[2]
    assert S % ATT_BLOCK == 0
    far_bucket = int(_rel_bucket_np(np.array(-(ATT_BLOCK + 1))))
    ii = np.arange(ATT_BLOCK)
    idx_prompt = np.stack([_rel_bucket_np(ii[:, None] - ii[None, :]),
                           _rel_bucket_np(ii[:, None] - ATT_BLOCK - ii[None, :])])
    bias_prompt = _bias_tiles(rel_bias_table, idx_prompt, far_bucket, LOG2E)
    qpos = P + np.arange(Ld)
    bias_past = _bias_tiles(rel_bias_table, _rel_bucket_np(np.arange(P)[None, :] - qpos[:, None])[None], None)
    bias_new = _bias_tiles(rel_bias_table, _rel_bucket_np(qpos[None, :] - qpos[:, None])[None], None)
    fg = final_g.reshape(1, D_MODEL)

    hp, hs = x_prompt, x_sample
    st_p, st_s = [], []
    for l in range(depth):
        p = _layer_params(l, norm1_g[l], w_in[l], da_lambda[l], da_subln_g[l], ml_conv_w[l], ml_conv_b[l],
                          ml_wq[l], ml_wk[l], ml_gate_b[l], ml_norm_g[l], ml_skip[l], cm_norm_g[l], cm_ws[l],
                          cm_b[l], w_out[l], norm2_g[l], peer_wq[l], peer_keys[l], peer_u[l], peer_v[l])
        last = l == depth - 1
        hp, sp = _layer(hp, None, p, bias_prompt, fg, last)
        past = (cache_k[l], cache_v[l], state_mlstm_c[l], state_mlstm_n[l], state_mlstm_m[l], state_mlstm_conv[l])
        hs, ss = _layer(hs, past, p, (bias_past, bias_new), fg, last)
        st_p.append(sp)
        st_s.append(ss)
    stack = lambda sts, i: jnp.stack([s[i] for s in sts])
    heads = lambda a, x: a.reshape(depth, x.shape[0], x.shape[1], DA_HEADS, DA_VDIM)
    return (hp, hs,
            heads(stack(st_p, 0), x_prompt), heads(stack(st_p, 1), x_prompt),
            stack(st_p, 2), stack(st_p, 3), stack(st_p, 4), stack(st_p, 5),
            heads(stack(st_s, 0), x_sample), heads(stack(st_s, 1), x_sample),
            stack(st_s, 2), stack(st_s, 3), stack(st_s, 4), stack(st_s, 5), stack(st_s, 6))
```

```python
import functools
import math

import numpy as np
import jax
import jax.numpy as jnp
from jax import lax
from jax.experimental import pallas as pl
from jax.experimental.pallas import tpu as pltpu

F32 = jnp.float32
BF16 = jnp.bfloat16

D_MODEL = 1024
CHUNK = 64
DA_HEADS = 4
DA_HEAD_DIM = 64
DA_VDIM = 128
DA_WIDTH = 512
ML_HEADS = 4
ML_HEAD_DIM = 64
ML_WIDTH = 256
ML_CONV = 4
CM_GROUPS = 4
CM_WIDTH = 256
CM_GROUP_DIM = 64
CM_CHUNK = 128
PEER_HEADS = 8
PEER_KEYS = 128
PEER_EXPERTS = PEER_KEYS * PEER_KEYS
PEER_QDIM = 256
PEER_HALF = 128
PEER_TOPK = 16
REL_BUCKETS = 32
REL_MAX_DIST = 128
EPS = 1e-6
NEG_INF = -1e30

LANES = 128
SUBLANES = 8
MXU_N = 256
ATT_BLOCK = MXU_N
FAR_BLOCKS = 2
LOG2E = 1.4426950408889634
ML_SEQS_PER_STEP = 4
ML_PAD = ML_HEADS * LANES
VMEM_LIMIT = 56 * 1024 * 1024
PEER_BLOCK_EXPERTS = 2048
PEER_CHUNK_EXPERTS = 512
PEER_ROWS_PER_SLAB = 1

_C_Q, _C_K, _C_V = 0, 512, 1024
_C_MC, _C_MV, _C_MO = 1536, 2048, 2560
_C_CU, _C_CV, _C_G = 3072, 3328, 3584
IN_PAD = 3712


def _cparams(sem):
    return pltpu.CompilerParams(dimension_semantics=sem, vmem_limit_bytes=VMEM_LIMIT)


def _gelu(x):
    return 0.5 * x * (1.0 + lax.erf(x * (2.0 ** -0.5)))


def _dot_nt(a, b):
    return lax.dot_general(a, b, (((1,), (1,)), ((), ())), preferred_element_type=F32)


def _dot_tn(a, b):
    return lax.dot_general(a, b, (((0,), (0,)), ((), ())), preferred_element_type=F32)


def _inproj_kernel(x_ref, g_ref, w_ref, q_ref, kf_ref, vf_ref, kb_ref, vb_ref,
                   mc_ref, mv_ref, mo_ref, cu_ref, cv_ref, gc_ref, *, blocked_t):
    x = x_ref[...]
    xn = x * lax.rsqrt(jnp.mean(x * x, axis=-1, keepdims=True) + EPS) * g_ref[...]
    xb = xn.astype(BF16)

    def proj(lo, hi):
        return jnp.dot(xb, w_ref[:, lo:hi], preferred_element_type=F32)

    def put(ref, val, scale=None):
        if scale is not None:
            val = val * scale
        if blocked_t:
            for n in range(ref.shape[0]):
                ref[n] = val[n * ATT_BLOCK:(n + 1) * ATT_BLOCK].T.astype(BF16)
        else:
            ref[...] = val.astype(BF16)

    put(q_ref, proj(_C_Q, _C_K), DA_HEAD_DIM ** -0.5 * (LOG2E if blocked_t else 1.0))
    k = proj(_C_K, _C_V)
    kf_ref[...] = k
    kb_ref[...] = k.astype(BF16)
    v = proj(_C_V, _C_MC)
    vf_ref[...] = v
    put(vb_ref, v)
    mc_ref[...] = proj(_C_MC, _C_MV)
    mv_ref[...] = proj(_C_MV, _C_MO)
    mo_ref[...] = proj(_C_MO, _C_CU)
    cu_ref[...] = proj(_C_CU, _C_CV)
    cv_ref[...] = proj(_C_CV, _C_G)
    gc_ref[...] = proj(_C_G, IN_PAD)


def _in_proj(x2d, g, w_pad, tm, blocked_t):
    T = x2d.shape[0]
    row = lambda n: pl.BlockSpec((tm, n), lambda i: (i, 0))
    full = lambda a: pl.BlockSpec(a.shape, lambda i: (0,) * a.ndim)
    if blocked_t:
        tspec = pl.BlockSpec((tm // ATT_BLOCK, DA_WIDTH, ATT_BLOCK), lambda i: (i, 0, 0))
        tshape = jax.ShapeDtypeStruct((T // ATT_BLOCK, DA_WIDTH, ATT_BLOCK), BF16)
    else:
        tspec, tshape = row(DA_WIDTH), jax.ShapeDtypeStruct((T, DA_WIDTH), BF16)
    outs = [(DA_WIDTH, F32), (DA_WIDTH, F32), (DA_WIDTH, BF16), None,
            (ML_PAD, F32), (ML_PAD, F32), (ML_PAD, F32), (CM_WIDTH, F32), (CM_WIDTH, F32), (LANES, F32)]
    specs = [tspec] + [tspec if o is None else row(o[0]) for o in outs]
    shapes = [tshape] + [tshape if o is None else jax.ShapeDtypeStruct((T, o[0]), o[1]) for o in outs]
    return pl.pallas_call(
        functools.partial(_inproj_kernel, blocked_t=blocked_t),
        grid=(T // tm,),
        in_specs=[row(D_MODEL), full(g), full(w_pad)],
        out_specs=specs,
        out_shape=shapes,
        compiler_params=_cparams(("parallel",)),
        name="in_proj",
    )(x2d, g, w_pad)


def _rel_bucket_np(rel):
    half = REL_BUCKETS // 2
    max_exact = half // 2
    ret = np.where(rel > 0, half, 0)
    n = np.abs(rel)
    nf = np.maximum(n, 1).astype(np.float32)
    large = max_exact + (np.log(nf / np.float32(max_exact)) / np.float32(math.log(REL_MAX_DIST / max_exact))
                         * np.float32(half - max_exact)).astype(np.int32)
    large = np.minimum(large, half - 1)
    return (ret + np.where(n < max_exact, n, large)).astype(np.int32)


def _bias_kernel(tab_ref, idx_ref, o_ref, *, shift_bucket, scale):
    for n in range(idx_ref.shape[0]):
        idx = idx_ref[n]
        for h in range(DA_HEADS):
            acc = jnp.zeros(idx.shape, F32)
            for b in range(REL_BUCKETS):
                acc = jnp.where(idx == b, tab_ref[b, h], acc)
            if shift_bucket is not None:
                acc = acc - tab_ref[shift_bucket, h]
            o_ref[n, h] = acc * scale


def _bias_tiles(table, idx_np, shift_bucket, scale=1.0):
    n, r, c = idx_np.shape
    return pl.pallas_call(
        functools.partial(_bias_kernel, shift_bucket=shift_bucket, scale=scale),
        in_specs=[pl.BlockSpec(memory_space=pltpu.SMEM), pl.BlockSpec(memory_space=pltpu.VMEM)],
        out_specs=pl.BlockSpec(memory_space=pltpu.VMEM),
        out_shape=jax.ShapeDtypeStruct((n, DA_HEADS, r, c), F32),
        name="rel_bias",
    )(table, jnp.asarray(idx_np))


def _lambda(lam_ref, lam_init):
    lp = lam_ref[...]
    a = jnp.sum(lp[0:1] * lp[1:2], axis=-1, keepdims=True)
    b = jnp.sum(lp[2:3] * lp[3:4], axis=-1, keepdims=True)
    return jnp.exp(a) - jnp.exp(b) + lam_init


def _subln(o1, l1, o2, l2, lam, g, lam_init):
    o = o1 / l1 - lam * (o2 / l2)
    o = o * lax.rsqrt(jnp.mean(o * o, axis=-1, keepdims=True) + EPS) * g
    return o * (1.0 - lam_init)


def _attn_prompt_kernel(lam_ref, qt_ref, k_ref, vt_ref, bias_ref, g_ref, o_ref,
                        m_sc, l_sc, a_sc, acc_sc, s_sc, p_sc, *, lam_init):
    tb = ATT_BLOCK
    qb = pl.program_id(1)
    lam = _lambda(lam_ref, lam_init)
    krow = lax.broadcasted_iota(jnp.int32, (tb, tb), 0)
    qcol = lax.broadcasted_iota(jnp.int32, (tb, tb), 1)
    diag_mask = (krow // CHUNK) <= (qcol // CHUNK)
    comp0 = lax.broadcasted_iota(jnp.int32, (LANES, 1), 0) < DA_HEAD_DIM
    has_prev = qb >= 1

    m_sc[...] = jnp.full(m_sc.shape, NEG_INF, F32)
    l_sc[...] = jnp.zeros(l_sc.shape, F32)
    acc_sc[...] = jnp.zeros(acc_sc.shape, F32)

    qts = []
    for h in range(DA_HEADS):
        qt = qt_ref[0, LANES * h:LANES * (h + 1), :]
        qts.append((jnp.where(comp0, qt, jnp.zeros_like(qt)), jnp.where(comp0, jnp.zeros_like(qt), qt)))

    def sweep(kblk, bias_n=None, mask=None, nk=1):
        rk = slice(0, nk * tb)
        for h in range(DA_HEADS):
            kh = k_ref[0, pl.ds(pl.multiple_of(kblk * tb, tb), nk * tb), LANES * h:LANES * (h + 1)]
            for c in range(2):
                s = jnp.dot(kh, qts[h][c], preferred_element_type=F32)
                if bias_n is not None:
                    s = s + bias_ref[bias_n, h]
                if mask is not None:
                    s = jnp.where(mask, s, NEG_INF)
                s_sc[2 * h + c, rk] = s
        for ch in range(2 * DA_HEADS):
            s = s_sc[ch, rk]
            m_old = m_sc[ch]
            m_new = jnp.maximum(m_old, jnp.max(s, axis=0, keepdims=True))
            alpha = jnp.exp2(m_old - m_new)
            p = jnp.exp2(s - m_new[0:1])
            l_sc[ch] = alpha * l_sc[ch] + jnp.sum(p, axis=0, keepdims=True)
            m_sc[ch] = m_new
            a_sc[ch] = alpha
            p_sc[ch, rk] = p.astype(BF16)
        for h in range(DA_HEADS):
            vth = [vt_ref[kblk + n, LANES * h:LANES * (h + 1), :] for n in range(nk)]
            vth = vth[0] if nk == 1 else jnp.concatenate(vth, axis=1)
            for c in range(2):
                ch = 2 * h + c
                acc_sc[ch] = a_sc[ch][0:1] * acc_sc[ch] + jnp.dot(vth, p_sc[ch, rk], preferred_element_type=F32)

    sweep(qb, 0, diag_mask)
    sweep(jnp.maximum(qb - 1, 0), 1, has_prev)
    nfar = jnp.maximum(qb - 1, 0)
    lax.fori_loop(0, nfar // FAR_BLOCKS, lambda j, _: (sweep(FAR_BLOCKS * j, nk=FAR_BLOCKS), 0)[1], 0)
    lax.fori_loop(nfar // FAR_BLOCKS * FAR_BLOCKS, nfar, lambda j, _: (sweep(j), 0)[1], 0)

    for h in range(DA_HEADS):
        o = acc_sc[2 * h] / l_sc[2 * h][0:1] - lam * (acc_sc[2 * h + 1] / l_sc[2 * h + 1][0:1])
        o = o * lax.rsqrt(jnp.mean(o * o, axis=0, keepdims=True) + EPS) * g_ref[...]
        o_ref[0, :, LANES * h:LANES * (h + 1)] = (o * (1.0 - lam_init)).T.astype(o_ref.dtype)


def _attn_prompt(qt, k, vt, bias, lam_p, g_t, lam_init, B):
    S = k.shape[1]
    tb = ATT_BLOCK
    nb = S // tb
    return pl.pallas_call(
        functools.partial(_attn_prompt_kernel, lam_init=lam_init),
        grid=(B, nb),
        in_specs=[
            pl.BlockSpec(lam_p.shape, lambda b, i: (0, 0)),
            pl.BlockSpec((1, DA_WIDTH, tb), lambda b, i: (b * nb + i, 0, 0)),
            pl.BlockSpec((1, S, DA_WIDTH), lambda b, i: (b, 0, 0)),
            pl.BlockSpec((nb, DA_WIDTH, tb), lambda b, i: (b, 0, 0)),
            pl.BlockSpec(bias.shape, lambda b, i: (0, 0, 0, 0)),
            pl.BlockSpec(g_t.shape, lambda b, i: (0, 0)),
        ],
        out_specs=pl.BlockSpec((1, tb, DA_WIDTH), lambda b, i: (b, i, 0)),
        out_shape=jax.ShapeDtypeStruct((B, S, DA_WIDTH), BF16),
        scratch_shapes=[pltpu.VMEM((2 * DA_HEADS, SUBLANES, tb), F32),
                        pltpu.VMEM((2 * DA_HEADS, SUBLANES, tb), F32),
                        pltpu.VMEM((2 * DA_HEADS, SUBLANES, tb), F32),
                        pltpu.VMEM((2 * DA_HEADS, LANES, tb), F32),
                        pltpu.VMEM((2 * DA_HEADS, FAR_BLOCKS * tb, tb), F32),
                        pltpu.VMEM((2 * DA_HEADS, FAR_BLOCKS * tb, tb), BF16)],
        compiler_params=_cparams(("parallel", "arbitrary")),
        name="attn_prompt",
    )(lam_p, qt, k, vt, bias, g_t)


def _attn_step_kernel(lam_ref, q_ref, kn_ref, vn_ref, kp_ref, vp_ref, bp_ref, bn_ref, g_ref, o_ref, *, lam_init):
    lam = _lambda(lam_ref, lam_init)
    first = lax.broadcasted_iota(jnp.int32, (1, LANES), 1) < DA_HEAD_DIM
    g = g_ref[...]
    for h in range(DA_HEADS):
        sl = slice(LANES * h, LANES * (h + 1))
        q = q_ref[0, :, sl]
        kn = kn_ref[0, :, sl]
        vn = vn_ref[0, :, sl]
        kp = kp_ref[0, :, sl].astype(BF16)
        vp = vp_ref[0, :, sl].astype(BF16)
        res = []
        for c in range(2):
            qc = jnp.where(first if c == 0 else jnp.logical_not(first), q, jnp.zeros_like(q))
            sp = _dot_nt(qc, kp) + bp_ref[0, h]
            sn = _dot_nt(qc, kn) + bn_ref[0, h]
            m = jnp.maximum(jnp.max(sp, axis=-1, keepdims=True), jnp.max(sn, axis=-1, keepdims=True))
            pp = jnp.exp(sp - m)
            pn = jnp.exp(sn - m)
            l = jnp.sum(pp, axis=-1, keepdims=True) + jnp.sum(pn, axis=-1, keepdims=True)
            acc = (jnp.dot(pp.astype(BF16), vp, preferred_element_type=F32)
                   + jnp.dot(pn.astype(BF16), vn, preferred_element_type=F32))
            res.append((l, acc))
        (l1, a1), (l2, a2) = res
        o_ref[0, :, sl] = _subln(a1, l1, a2, l2, lam, g, lam_init).astype(o_ref.dtype)


def _attn_step(q, kn, vn, kp, vp, bias_p, bias_n, lam_p, g, lam_init):
    B, L, _ = q.shape
    P = kp.shape[1]
    c3 = lambda b: (b, 0, 0)
    z4 = lambda b: (0, 0, 0, 0)
    return pl.pallas_call(
        functools.partial(_attn_step_kernel, lam_init=lam_init),
        grid=(B,),
        in_specs=[
            pl.BlockSpec(lam_p.shape, lambda b: (0, 0)),
            pl.BlockSpec((1, L, DA_WIDTH), c3), pl.BlockSpec((1, L, DA_WIDTH), c3), pl.BlockSpec((1, L, DA_WIDTH), c3),
            pl.BlockSpec((1, P, DA_WIDTH), c3), pl.BlockSpec((1, P, DA_WIDTH), c3),
            pl.BlockSpec(bias_p.shape, z4), pl.BlockSpec(bias_n.shape, z4),
            pl.BlockSpec(g.shape, lambda b: (0, 0)),
        ],
        out_specs=pl.BlockSpec((1, L, DA_WIDTH), c3),
        out_shape=jax.ShapeDtypeStruct((B, L, DA_WIDTH), BF16),
        compiler_params=_cparams(("parallel",)),
        name="attn_step",
    )(lam_p, q, kn, vn, kp, vp, bias_p, bias_n, g)


def _log_sigmoid(x):
    return jnp.minimum(x, 0.0) - jnp.log(1.0 + jnp.exp(-jnp.abs(x)))


def _mlstm_kernel(gb_ref, mc_ref, mv_ref, mo_ref, gc_ref, cw_ref, cb_ref, wq_ref, wk_ref, ng_ref, sk_ref,
                  c0_ref, n0_ref, m0_ref, cv0_ref,
                  o_ref, co_ref, no_ref, mo2_ref, cvo_ref,
                  xbuf, c_sc, n_sc, m_sc):
    j = pl.program_id(1)
    L = CHUNK
    bb = mc_ref.shape[0]
    chains = [(b, h) for b in range(bb) for h in range(ML_HEADS)]
    hsl = lambda h: slice(LANES * h, LANES * (h + 1))

    @pl.when(j == 0)
    def _():
        xbuf[:, 5:8, :] = cv0_ref[...]
        c_sc[...] = c0_ref[...]
        n_sc[...] = n0_ref[...]
        m_sc[...] = m0_ref[...]

    r = lax.broadcasted_iota(jnp.int32, (L, L), 0)
    s = lax.broadcasted_iota(jnp.int32, (L, L), 1)
    tril = s <= r

    cc, gcol, gt = [], [], []
    for b in range(bb):
        x = mc_ref[b]
        xbuf[b, 8:8 + L, :] = x
        y = (cb_ref[...] + cw_ref[3:4, :] * x + cw_ref[2:3, :] * xbuf[b, 7:7 + L, :]
             + cw_ref[1:2, :] * xbuf[b, 6:6 + L, :] + cw_ref[0:1, :] * xbuf[b, 5:5 + L, :])
        cc.append(y * jax.nn.sigmoid(y))
        tail = xbuf[b, 5 + L:8 + L, :]
        xbuf[b, 5:8, :] = tail
        cvo_ref[b] = tail
        g = gc_ref[b]
        gcol.append(g)
        gt.append(jnp.concatenate([g, jnp.zeros((LANES - L, LANES), F32)], axis=0).T)

    qs, ks = {}, {}
    for b, h in chains:
        cch = cc[b][:, hsl(h)]
        qs[b, h] = jnp.dot(cch, wq_ref[h], preferred_element_type=F32)
        ks[b, h] = jnp.dot(cch, wk_ref[h], preferred_element_type=F32) * (ML_HEAD_DIM ** -0.5)
    sc, rd = {}, {}
    for b, h in chains:
        sc[b, h] = _dot_nt(qs[b, h], ks[b, h])
        rd[b, h] = jnp.dot(qs[b, h], c_sc[b, h], preferred_element_type=F32)
    ig_c, ig_r, lf_c, lf_r = {}, {}, {}, {}
    for b, h in chains:
        b_i = gb_ref[0, h]
        b_f = gb_ref[1, h]
        ig_c[b, h] = gcol[b][:, h:h + 1] + b_i
        lf_c[b, h] = _log_sigmoid(gcol[b][:, 4 + h:5 + h] + b_f)
        ig_r[b, h] = gt[b][h:h + 1, 0:L] + b_i
        lf_r[b, h] = _log_sigmoid(gt[b][4 + h:5 + h, 0:L] + b_f)
    f_c, f_r = {}, {}
    for ch in chains:
        f_c[ch] = jnp.sum(jnp.where(tril, lf_r[ch], 0.0), axis=1, keepdims=True)
        f_r[ch] = jnp.sum(jnp.where(r <= s, lf_c[ch], 0.0), axis=0, keepdims=True)
    dmat, dmax, tmax, m_prev, fl = {}, {}, {}, {}, {}
    for b, h in chains:
        ch = (b, h)
        m_prev[ch] = m_sc[b, h][0:1, 0:1]
        fl[ch] = f_c[ch][L - 1:L, :]
        dmat[ch] = jnp.where(tril, f_c[ch] - f_r[ch] + ig_r[ch], NEG_INF)
        dmax[ch] = jnp.max(dmat[ch], axis=1, keepdims=True)
        tmax[ch] = jnp.max(fl[ch] - f_r[ch] + ig_r[ch], axis=1, keepdims=True)
    sw, iw, m_t, kw, wc, m_new = {}, {}, {}, {}, {}, {}
    for ch in chains:
        inter = f_c[ch] + m_prev[ch]
        m_t[ch] = jnp.maximum(inter, dmax[ch])
        sw[ch] = sc[ch] * jnp.exp(dmat[ch] - m_t[ch])
        iw[ch] = jnp.exp(inter - m_t[ch])
        m_new[ch] = jnp.maximum(fl[ch] + m_prev[ch], tmax[ch])
        wc[ch] = jnp.exp(fl[ch] + m_prev[ch] - m_new[ch])
        kw[ch] = ks[ch] * jnp.exp(fl[ch] - f_c[ch] + ig_c[ch] - m_new[ch])
    nv, up = {}, {}
    for b, h in chains:
        vh = mv_ref[b, :, hsl(h)]
        nv[b, h] = jnp.dot(sw[b, h], vh, preferred_element_type=F32)
        up[b, h] = _dot_tn(kw[b, h], vh)
    n_prev, den, ksum = {}, {}, {}
    for b, h in chains:
        ch = (b, h)
        n_prev[ch] = n_sc[b, h]
        den[ch] = (jnp.sum(sw[ch], axis=1, keepdims=True)
                   + iw[ch] * jnp.sum(qs[ch] * n_prev[ch], axis=1, keepdims=True))
        ksum[ch] = jnp.sum(kw[ch], axis=0, keepdims=True)
    hout, ssq = {}, {}
    for ch in chains:
        hout[ch] = (nv[ch] + iw[ch] * rd[ch]) / jnp.maximum(jnp.abs(den[ch]), jnp.exp(-m_t[ch]))
        ssq[ch] = jnp.sum(hout[ch] * hout[ch], axis=1, keepdims=True)
    for b, h in chains:
        ch = (b, h)
        c_new = wc[ch] * c_sc[b, h] + up[ch]
        n_new = wc[ch] * n_prev[ch] + ksum[ch]
        m_row = jnp.broadcast_to(m_new[ch], (1, LANES))
        c_sc[b, h] = c_new
        n_sc[b, h] = n_new
        m_sc[b, h] = m_row
        co_ref[b, h] = c_new
        no_ref[b, h] = n_new
        mo2_ref[b, h] = m_row
        hn = hout[ch] * lax.rsqrt(ssq[ch] * (1.0 / ML_HEAD_DIM) + EPS) * ng_ref[:, hsl(h)]
        o_ref[b, :, hsl(h)] = ((hn + sk_ref[:, hsl(h)] * cc[b][:, hsl(h)])
                               * jax.nn.sigmoid(mo_ref[b, :, hsl(h)])).astype(o_ref.dtype)


def _mlstm(gate_b, mc, mv, mo, gc, cw, cb, wq, wk, ng, sk, c0, n0, m0, cv0, bb):
    B, L, _ = mc.shape
    nc = L // CHUNK
    tok = lambda n: pl.BlockSpec((bb, CHUNK, n), lambda b, j: (b, j, 0))
    full = lambda a: pl.BlockSpec(a.shape, lambda b, j: (0,) * a.ndim)
    st4 = lambda a: pl.BlockSpec((bb,) + a.shape[1:], lambda b, j: (b,) + (0,) * (a.ndim - 1))
    return pl.pallas_call(
        _mlstm_kernel,
        grid=(B // bb, nc),
        in_specs=[pl.BlockSpec(memory_space=pltpu.SMEM),
                  tok(ML_PAD), tok(ML_PAD), tok(ML_PAD), tok(LANES),
                  full(cw), full(cb), full(wq), full(wk), full(ng), full(sk),
                  st4(c0), st4(n0), st4(m0), st4(cv0)],
        out_specs=[tok(ML_PAD), st4(c0), st4(n0), st4(m0), st4(cv0)],
        out_shape=[jax.ShapeDtypeStruct((B, L, ML_PAD), BF16),
                   jax.ShapeDtypeStruct(c0.shape, F32), jax.ShapeDtypeStruct(n0.shape, F32),
                   jax.ShapeDtypeStruct(m0.shape, F32), jax.ShapeDtypeStruct(cv0.shape, F32)],
        scratch_shapes=[pltpu.VMEM((bb, 8 + CHUNK, ML_PAD), F32),
                        pltpu.VMEM((bb, ML_HEADS, LANES, LANES), F32),
                        pltpu.VMEM((bb, ML_HEADS, 1, LANES), F32),
                        pltpu.VMEM((bb, ML_HEADS, 1, LANES), F32)],
        compiler_params=_cparams(("parallel", "arbitrary")),
        name="mlstm",
    )(gate_b, mc, mv, mo, gc, cw, cb, wq, wk, ng, sk, c0, n0, m0, cv0)


def _mixout_kernel(x_ref, oda_ref, oml_ref, cu_ref, cv_ref, cmg_ref, ws_ref, bm_ref,
                   wda_ref, wml_ref, wcm_ref, x1_ref, vcm_ref, *, lc):
    tm = x_ref.shape[0]
    u = _gelu(cu_ref[...])
    gv = _gelu(cv_ref[...])
    vcm = gv * lax.rsqrt(jnp.mean(gv * gv, axis=-1, keepdims=True) + EPS) * cmg_ref[...]
    vcm_ref[...] = vcm
    r = lax.broadcasted_iota(jnp.int32, (lc, lc), 0)
    c = lax.broadcasted_iota(jnp.int32, (lc, lc), 1)
    grp = lax.broadcasted_iota(jnp.int32, (1, CM_WIDTH), 1) // CM_GROUP_DIM
    wsm = [jnp.where(c <= r, ws_ref[g, 0:lc, 0:lc], 0.0).astype(BF16) for g in range(CM_GROUPS)]
    pieces = []
    for ci in range(tm // lc):
        vch = vcm[ci * lc:(ci + 1) * lc].astype(BF16)
        mixed = bm_ref[0:lc, :]
        for g in range(CM_GROUPS):
            mixed = mixed + jnp.where(grp == g, jnp.dot(wsm[g], vch, preferred_element_type=F32), 0.0)
        pieces.append(u[ci * lc:(ci + 1) * lc] * mixed)
    ocm = jnp.concatenate(pieces, axis=0) if len(pieces) > 1 else pieces[0]
    y = (jnp.dot(oda_ref[...], wda_ref[...], preferred_element_type=F32)
         + jnp.dot(oml_ref[...], wml_ref[...], preferred_element_type=F32)
         + jnp.dot(ocm.astype(BF16), wcm_ref[...], preferred_element_type=F32))
    x1_ref[...] = x_ref[...] + y


def _mix_out(x2d, oda, oml, cu, cv, cmg, ws, bmat, wda, wml, wcm, tm, lc):
    T = x2d.shape[0]
    row = lambda n: pl.BlockSpec((tm, n), lambda i: (i, 0))
    full = lambda a: pl.BlockSpec(a.shape, lambda i: (0,) * a.ndim)
    return pl.pallas_call(
        functools.partial(_mixout_kernel, lc=lc),
        grid=(T // tm,),
        in_specs=[row(D_MODEL), row(DA_WIDTH), row(ML_PAD), row(CM_WIDTH), row(CM_WIDTH),
                  full(cmg), full(ws), full(bmat), full(wda), full(wml), full(wcm)],
        out_specs=[row(D_MODEL), row(CM_WIDTH)],
        out_shape=[jax.ShapeDtypeStruct((T, D_MODEL), F32), jax.ShapeDtypeStruct((T, CM_WIDTH), F32)],
        compiler_params=_cparams(("parallel",)),
        name="mix_out",
    )(x2d, oda, oml, cu, cv, cmg, ws, bmat, wda, wml, wcm)


_NTOP = PEER_TOPK + 1
_PAIRS = [(a, b) for a in range(_NTOP) for b in range(_NTOP) if (a + 1) * (b + 1) <= _NTOP]
_NCAND = ((len(_PAIRS) + 7) // 8) * 8


def _merge_exchange_network(n):
    pairs = []
    t = max(1, math.ceil(math.log2(n)))
    p = 1 << (t - 1)
    while p > 0:
        q, r, d = 1 << (t - 1), 0, p
        while d > 0:
            pairs += [(i, i + d) for i in range(n - d) if (i & p) == r]
            d, q, r = q - p, q >> 1, p
        p >>= 1
    return pairs


_KEY_SLABS = PEER_KEYS // SUBLANES
_SORT_NET = _merge_exchange_network(_KEY_SLABS)


def _peer_select_lanes(st_ref, a_ref, b_ref, bd_ref, vals, cand, ls):
    tt = ls.stop - ls.start
    for h in range(PEER_HEADS):
        for c in range(2):
            v = [st_ref[h, c, SUBLANES * j:SUBLANES * (j + 1), ls] for j in range(_KEY_SLABS)]
            for i, j in _SORT_NET:
                v[i], v[j] = jnp.maximum(v[i], v[j]), jnp.minimum(v[i], v[j])
            for i in range(_NTOP):
                m = jnp.max(v[0], axis=0, keepdims=True)
                vals[c, i:i + 1, :] = m
                keep = _NTOP - 1 - i
                popped = v[0] == m
                for j in range(min(keep, _KEY_SLABS)):
                    v[j] = jnp.where(popped, v[j + 1] if j + 1 < _KEY_SLABS else NEG_INF, v[j])
        cand[...] = jnp.full((_NCAND, tt), NEG_INF, F32)
        for i, (a, b) in enumerate(_PAIRS):
            cand[i:i + 1, :] = vals[0, a:a + 1, :] + vals[1, b:b + 1, :]
        top1 = vals[0, 0:1, :]
        top2 = vals[1, 0:1, :]
        best = top1 + top2
        cur = cand[...]
        z = jnp.zeros((1, tt), F32)
        c16 = best
        for i in range(PEER_TOPK):
            c16 = jnp.max(cur, axis=0, keepdims=True)
            z = z + jnp.exp(c16 - best)
            cur = jnp.where(cur == c16, NEG_INF, cur)
        thr = 0.5 * (c16 + jnp.max(cur, axis=0, keepdims=True))
        s1 = st_ref[h, 0, :, ls]
        a_ref[h, :, ls] = 0.5 * jnp.exp(s1 - top1) / z
        b_ref[h, :, ls] = jnp.exp(st_ref[h, 1, :, ls] - top2)
        bd_ref[h, :, ls] = jnp.exp(thr - s1 - top2)


def _peer_route_kernel(x_ref, g_ref, wq_ref, keys_ref, xnt_ref, a_ref, b_ref, bd_ref, st_sc, vals, cand):
    x = x_ref[...]
    xf = x * lax.rsqrt(jnp.mean(x * x, axis=-1, keepdims=True) + EPS) * g_ref[...]
    xn = xf.astype(BF16)
    xnt_ref[...] = xf.T.astype(BF16)
    qs = [jnp.dot(xn, wq_ref[:, PEER_QDIM * h:PEER_QDIM * (h + 1)], preferred_element_type=F32)
          for h in range(PEER_HEADS)]
    qs = [(q * lax.rsqrt(jnp.mean(q * q, axis=-1, keepdims=True) + EPS)).astype(BF16) for q in qs]
    for h in range(PEER_HEADS):
        for c in range(2):
            qc = qs[h][:, PEER_HALF * c:PEER_HALF * (c + 1)]
            st_sc[h, c] = _dot_nt(keys_ref[h, c], qc)
    for lt in range(x_ref.shape[0] // LANES):
        _peer_select_lanes(st_sc, a_ref, b_ref, bd_ref, vals, cand, slice(lt * LANES, (lt + 1) * LANES))


def _peer_route(x1, g, wq, keys, tm):
    T = x1.shape[0]
    full = lambda a: pl.BlockSpec(a.shape, lambda i: (0,) * a.ndim)
    spec = pl.BlockSpec((PEER_HEADS, PEER_KEYS, tm), lambda i: (0, 0, i))
    shp = jax.ShapeDtypeStruct((PEER_HEADS, PEER_KEYS, T), F32)
    return pl.pallas_call(
        _peer_route_kernel,
        grid=(T // tm,),
        in_specs=[pl.BlockSpec((tm, D_MODEL), lambda i: (i, 0)), full(g), full(wq), full(keys)],
        out_specs=[pl.BlockSpec((D_MODEL, tm), lambda i: (0, i)), spec, spec, spec],
        out_shape=[jax.ShapeDtypeStruct((D_MODEL, T), BF16), shp, shp, shp],
        scratch_shapes=[pltpu.VMEM((PEER_HEADS, 2, PEER_KEYS, tm), F32),
                        pltpu.VMEM((2, 3 * SUBLANES, LANES), F32), pltpu.VMEM((_NCAND, LANES), F32)],
        compiler_params=_cparams(("parallel",)),
        name="peer_route",
    )(x1, g, wq, keys)


def _peer_dense_kernel(x1_ref, xnt_ref, a_ref, b_ref, bd_ref, u_ref, vt_ref, fg_ref, o_ref,
                       acc_ref, *scr, rows, final_norm):
    nchunk = len(scr) // 2
    ht_refs, at_refs = scr[:nchunk], scr[nchunk:]
    j = pl.program_id(1)
    last = pl.num_programs(1) - 1

    @pl.when(j == 0)
    def _():
        acc_ref[...] = jnp.zeros_like(acc_ref)

    ch, tt = ht_refs[0].shape
    per = PEER_ROWS_PER_SLAB
    assert nchunk * ch == rows * PEER_KEYS
    lw = LANES
    assert SUBLANES % per == 0 and (ch // PEER_KEYS) % per == 0

    def proj(c):
        cs = slice(c * ch, (c + 1) * ch)
        ht_refs[c][...] = jnp.dot(u_ref[cs, :], xnt_ref[...], preferred_element_type=F32)

    def gate(c):
        for p in range(ch // PEER_KEYS // per):
            gate_rows(c, range(c * (ch // PEER_KEYS) + p * per, c * (ch // PEER_KEYS) + (p + 1) * per))

    def gate_rows(c, rr):
        heads = range(PEER_HEADS)
        for lt in range(tt // lw):
            ls = slice(lt * lw, (lt + 1) * lw)
            i0 = pl.multiple_of(j * rows + rr[0] // SUBLANES * SUBLANES, SUBLANES)
            sub = [r % SUBLANES for r in rr]
            att = [a_ref[h, pl.ds(i0, SUBLANES), ls] for h in heads]
            bdt = [bd_ref[h, pl.ds(i0, SUBLANES), ls] for h in heads]
            av = [[jnp.broadcast_to(att[h][q:q + 1], (SUBLANES, lw)) for h in heads] for q in sub]
            bdv = [[jnp.broadcast_to(bdt[h][q:q + 1], (SUBLANES, lw)) for h in heads] for q in sub]
            for sg in range(PEER_KEYS // (2 * SUBLANES)):
                halves = [[] for _ in rr]
                for half in range(2):
                    k0 = sg * 2 * SUBLANES + half * SUBLANES
                    bv = [b_ref[h, k0:k0 + SUBLANES, ls] for h in heads]
                    for ri, r in enumerate(rr):
                        t = [jnp.where(bv[h] >= bdv[ri][h], bv[h], 0.0) * av[ri][h] for h in heads]
                        while len(t) > 1:
                            t = [t[i] + t[i + 1] for i in range(0, len(t), 2)]
                        h0 = r * PEER_KEYS - c * ch + k0
                        ht = ht_refs[c][h0:h0 + SUBLANES, ls]
                        halves[ri].append(t[0] * (ht * (1.0 + lax.erf(ht * (2.0 ** -0.5)))))
                for ri, r in enumerate(rr):
                    o0 = r * PEER_KEYS - c * ch + sg * 2 * SUBLANES
                    at_refs[c][o0:o0 + 2 * SUBLANES, ls] = jnp.concatenate(halves[ri], axis=0).astype(BF16)

    def value(c):
        acc_ref[...] += jnp.dot(vt_ref[:, c * ch:(c + 1) * ch], at_refs[c][...], preferred_element_type=F32)

    proj(0)
    for c in range(nchunk):
        if c + 1 < nchunk:
            proj(c + 1)
        gate(c)
        value(c)

    @pl.when(j == last)
    def _():
        x2 = x1_ref[...] + acc_ref[...].T
        if final_norm:
            x2 = x2 * lax.rsqrt(jnp.mean(x2 * x2, axis=-1, keepdims=True) + EPS) * fg_ref[...]
        o_ref[...] = x2


def _peer_dense(x1, xn, a, b, bd, u_bf, vt_bf, fg, tt, eb, final_norm):
    T = x1.shape[0]
    rows = eb // PEER_KEYS
    ne = PEER_EXPERTS // eb
    hk = pl.BlockSpec((PEER_HEADS, PEER_KEYS, tt), lambda i, j: (0, 0, i))
    return pl.pallas_call(
        functools.partial(_peer_dense_kernel, rows=rows, final_norm=final_norm),
        grid=(T // tt, ne),
        in_specs=[pl.BlockSpec((tt, D_MODEL), lambda i, j: (i, 0)),
                  pl.BlockSpec((D_MODEL, tt), lambda i, j: (0, i)),
                  hk, hk, hk,
                  pl.BlockSpec((eb, D_MODEL), lambda i, j: (j, 0)),
                  pl.BlockSpec((D_MODEL, eb), lambda i, j: (0, j)),
                  pl.BlockSpec(fg.shape, lambda i, j: (0, 0))],
        out_specs=pl.BlockSpec((tt, D_MODEL), lambda i, j: (i, 0)),
        out_shape=jax.ShapeDtypeStruct((T, D_MODEL), F32),
        scratch_shapes=([pltpu.VMEM((D_MODEL, tt), F32)]
                        + [pltpu.VMEM((PEER_CHUNK_EXPERTS, tt), F32)] * (eb // PEER_CHUNK_EXPERTS)
                        + [pltpu.VMEM((PEER_CHUNK_EXPERTS, tt), BF16)] * (eb // PEER_CHUNK_EXPERTS)),
        compiler_params=_cparams(("parallel", "arbitrary")),
        name="peer_dense",
    )(x1, xn, a, b, bd, u_bf, vt_bf, fg)


def _pad_heads(w, axis=-1):
    w = jnp.moveaxis(w, axis, -1)
    lead = w.shape[:-1]
    w = w.reshape(lead + (ML_HEADS, ML_HEAD_DIM))
    w = jnp.pad(w, [(0, 0)] * len(lead) + [(0, 0), (0, LANES - ML_HEAD_DIM)])
    return jnp.moveaxis(w.reshape(lead + (ML_PAD,)), -1, axis)


def _layer_params(l, norm1_g, w_in, da_lambda, da_subln_g, ml_conv_w, ml_conv_b, ml_wq, ml_wk, ml_gate_b,
                  ml_norm_g, ml_skip, cm_norm_g, cm_ws, cm_b, w_out, norm2_g, peer_wq, peer_keys, peer_u, peer_v):
    o = np.cumsum((0, 512, 512, 512, 256, 256, 256, 4, 4, 256, 256))
    seg = lambda i: w_in[:, o[i]:o[i + 1]]
    gates = jnp.pad(jnp.concatenate([seg(6), seg(7)], axis=1), ((0, 0), (0, LANES - 2 * ML_HEADS)))
    w_pad = jnp.concatenate([seg(0), seg(1), seg(2), _pad_heads(seg(3)), _pad_heads(seg(4)), _pad_heads(seg(5)),
                             seg(8), seg(9), gates], axis=1).astype(BF16)
    pad_sq = lambda w: jnp.pad(w, ((0, 0), (0, LANES - ML_HEAD_DIM), (0, LANES - ML_HEAD_DIM)))
    return dict(
        lam_init=0.8 - 0.6 * math.exp(-0.3 * l),
        g1=norm1_g.reshape(1, D_MODEL), w_pad=w_pad,
        lam_p=da_lambda, subln_g=da_subln_g.reshape(1, DA_VDIM),
        cw=_pad_heads(ml_conv_w), cb=_pad_heads(ml_conv_b.reshape(1, ML_WIDTH)),
        wq=pad_sq(ml_wq), wk=pad_sq(ml_wk), gate_b=ml_gate_b,
        ng=_pad_heads(ml_norm_g.reshape(1, ML_WIDTH)), sk=_pad_heads(ml_skip.reshape(1, ML_WIDTH)),
        cmg=cm_norm_g.reshape(1, CM_WIDTH), ws=cm_ws,
        bmat=jnp.repeat(cm_b.T, CM_GROUP_DIM, axis=1),
        wda=w_out[:DA_WIDTH].astype(BF16),
        wml=_pad_heads(w_out[DA_WIDTH:DA_WIDTH + ML_WIDTH], axis=0).astype(BF16),
        wcm=w_out[DA_WIDTH + ML_WIDTH:].astype(BF16),
        g2=norm2_g.reshape(1, D_MODEL), pwq=peer_wq.astype(BF16), pkeys=peer_keys.astype(BF16),
        u_bf=peer_u.astype(BF16), vt_bf=peer_v.T.astype(BF16),
    )


def _pick(n, prefs):
    for p in prefs:
        if n % p == 0:
            return p
    return n


def _layer(x, past, p, bias, final_g, final_norm):
    B, L, _ = x.shape
    T = B * L
    x2d = x.reshape(T, D_MODEL)
    tm = _pick(T, (512, 256, 128))
    q, kf, vf, kb, vb, mc, mv, mo, cu, cv, gc = _in_proj(x2d, p["g1"], p["w_pad"], tm, past is None)
    r3 = lambda a: a.reshape(B, L, a.shape[-1])
    if past is None:
        g_t = jnp.broadcast_to(p["subln_g"].reshape(DA_VDIM, 1), (DA_VDIM, ATT_BLOCK))
        oda = _attn_prompt(q, r3(kb), vb, bias, p["lam_p"], g_t, p["lam_init"], B)
        c0 = jnp.zeros((B, ML_HEADS, LANES, LANES), F32)
        n0 = jnp.zeros((B, ML_HEADS, 1, LANES), F32)
        m0 = jnp.zeros((B, ML_HEADS, 1, LANES), F32)
        cv0 = jnp.zeros((B, ML_CONV - 1, ML_PAD), F32)
        lc = CM_CHUNK
    else:
        pk, pv, pc, pn, pm, pconv = past
        P = pk.shape[1]
        oda = _attn_step(r3(q), r3(kb), r3(vb), pk.reshape(B, P, DA_WIDTH), pv.reshape(B, P, DA_WIDTH),
                         bias[0], bias[1], p["lam_p"], p["subln_g"], p["lam_init"])
        c0 = jnp.pad(pc, ((0, 0), (0, 0), (0, LANES - ML_HEAD_DIM), (0, LANES - ML_HEAD_DIM)))
        n0 = jnp.pad(pn, ((0, 0), (0, 0), (0, LANES - ML_HEAD_DIM)))[:, :, None, :]
        m0 = jnp.broadcast_to(pm[:, :, None, None], (B, ML_HEADS, 1, LANES))
        cv0 = _pad_heads(pconv)
        lc = L
    oml, c_new, n_new, m_new, conv_new = _mlstm(
        p["gate_b"], r3(mc), r3(mv), r3(mo), r3(gc), p["cw"], p["cb"], p["wq"], p["wk"], p["ng"], p["sk"],
        c0, n0, m0, cv0, ML_SEQS_PER_STEP)
    x1, vcm = _mix_out(x2d, oda.reshape(T, DA_WIDTH), oml.reshape(T, ML_PAD), cu, cv, p["cmg"], p["ws"], p["bmat"],
                       p["wda"], p["wml"], p["wcm"], _pick(T, (512, 256, 128)), lc)
    xn2, a, b, bd = _peer_route(x1, p["g2"], p["pwq"], p["pkeys"], _pick(T, (256, 128)))
    x2 = _peer_dense(x1, xn2, a, b, bd, p["u_bf"], p["vt_bf"], final_g, _pick(T, (512, 256, 128)), PEER_BLOCK_EXPERTS, final_norm)
    state = (kf, vf,

             c_new[:, :, :ML_HEAD_DIM, :ML_HEAD_DIM], n_new[:, :, 0, :ML_HEAD_DIM], m_new[:, :, 0, 0],
             conv_new.reshape(B, ML_CONV - 1, ML_HEADS, LANES)[..., :ML_HEAD_DIM].reshape(B, ML_CONV - 1, ML_WIDTH),
             vcm.reshape(B, L, CM_WIDTH))
    return x2.reshape(B, L, D_MODEL), state


def kernel(x_prompt, x_sample, cache_k, cache_v, state_mlstm_c, state_mlstm_n, state_mlstm_m, state_mlstm_conv, norm1_g, w_in, da_lambda, da_subln_g, rel_bias_table, ml_conv_w, ml_conv_b, ml_wq, ml_wk, ml_gate_b, ml_norm_g, ml_skip, cm_norm_g, cm_ws, cm_b, w_out, norm2_g, peer_wq, peer_keys, peer_u, peer_v, final_g):
    depth = w_in.shape[0]
    S = x_prompt.shape[1]
    Ld = x_sample.shape[1]
    P = cache_k.shape[2]
    assert S % ATT_BLOCK == 0
    far_bucket = int(_rel_bucket_np(np.array(-(ATT_BLOCK + 1))))
    ii = np.arange(ATT_BLOCK)
    idx_prompt = np.stack([_rel_bucket_np(ii[:, None] - ii[None, :]),
                           _rel_bucket_np(ii[:, None] - ATT_BLOCK - ii[None, :])])
    bias_prompt = _bias_tiles(rel_bias_table, idx_prompt, far_bucket, LOG2E)
    qpos = P + np.arange(Ld)
    bias_past = _bias_tiles(rel_bias_table, _rel_bucket_np(np.arange(P)[None, :] - qpos[:, None])[None], None)
    bias_new = _bias_tiles(rel_bias_table, _rel_bucket_np(qpos[None, :] - qpos[:, None])[None], None)
    fg = final_g.reshape(1, D_MODEL)

    hp, hs = x_prompt, x_sample
    st_p, st_s = [], []
    for l in range(depth):
        p = _layer_params(l, norm1_g[l], w_in[l], da_lambda[l], da_subln_g[l], ml_conv_w[l], ml_conv_b[l],
                          ml_wq[l], ml_wk[l], ml_gate_b[l], ml_norm_g[l], ml_skip[l], cm_norm_g[l], cm_ws[l],
                          cm_b[l], w_out[l], norm2_g[l], peer_wq[l], peer_keys[l], peer_u[l], peer_v[l])
        last = l == depth - 1
        hp, sp = _layer(hp, None, p, bias_prompt, fg, last)
        past = (cache_k[l], cache_v[l], state_mlstm_c[l], state_mlstm_n[l], state_mlstm_m[l], state_mlstm_conv[l])
        hs, ss = _layer(hs, past, p, (bias_past, bias_new), fg, last)
        st_p.append(sp)
        st_s.append(ss)
    stack = lambda sts, i: jnp.stack([s[i] for s in sts])
    heads = lambda a, x: a.reshape(depth, x.shape[0], x.shape[1], DA_HEADS, DA_VDIM)
    return (hp, hs,
            heads(stack(st_p, 0), x_prompt), heads(stack(st_p, 1), x_prompt),
            stack(st_p, 2), stack(st_p, 3), stack(st_p, 4), stack(st_p, 5),
            heads(stack(st_s, 0), x_sample), heads(stack(st_s, 1), x_sample),
            stack(st_s, 2), stack(st_s, 3), stack(st_s, 4), stack(st_s, 5), stack(st_s, 6))
```

```python
import functools
import math

import numpy as np
import jax
import jax.numpy as jnp
from jax import lax
from jax.experimental import pallas as pl
from jax.experimental.pallas import tpu as pltpu

F32 = jnp.float32
BF16 = jnp.bfloat16

D_MODEL = 1024
CHUNK = 64
DA_HEADS = 4
DA_HEAD_DIM = 64
DA_VDIM = 128
DA_WIDTH = 512
ML_HEADS = 4
ML_HEAD_DIM = 64
ML_WIDTH = 256
ML_CONV = 4
CM_GROUPS = 4
CM_WIDTH = 256
CM_GROUP_DIM = 64
CM_CHUNK = 128
PEER_HEADS = 8
PEER_KEYS = 128
PEER_EXPERTS = PEER_KEYS * PEER_KEYS
PEER_QDIM = 256
PEER_HALF = 128
PEER_TOPK = 16
REL_BUCKETS = 32
REL_MAX_DIST = 128
EPS = 1e-6
NEG_INF = -1e30

LANES = 128
SUBLANES = 8
MXU_N = 256
ATT_BLOCK = MXU_N
FAR_BLOCKS = 2
LOG2E = 1.4426950408889634
ML_SEQS_PER_STEP = 4
ML_PAD = ML_HEADS * LANES
VMEM_LIMIT = 56 * 1024 * 1024
PEER_BLOCK_EXPERTS = 2048
PEER_CHUNK_EXPERTS = 512
PEER_ROWS_PER_SLAB = 1

_C_Q, _C_K, _C_V = 0, 512, 1024
_C_MC, _C_MV, _C_MO = 1536, 2048, 2560
_C_CU, _C_CV, _C_G = 3072, 3328, 3584
IN_PAD = 3712


def _cparams(sem):
    return pltpu.CompilerParams(dimension_semantics=sem, vmem_limit_bytes=VMEM_LIMIT)


def _gelu(x):
    return 0.5 * x * (1.0 + lax.erf(x * (2.0 ** -0.5)))


def _dot_nt(a, b):
    return lax.dot_general(a, b, (((1,), (1,)), ((), ())), preferred_element_type=F32)


def _dot_tn(a, b):
    return lax.dot_general(a, b, (((0,), (0,)), ((), ())), preferred_element_type=F32)


def _inproj_kernel(x_ref, g_ref, w_ref, q_ref, kf_ref, vf_ref, kb_ref, vb_ref,
                   mc_ref, mv_ref, mo_ref, cu_ref, cv_ref, gc_ref, *, blocked_t):
    x = x_ref[...]
    xn = x * lax.rsqrt(jnp.mean(x * x, axis=-1, keepdims=True) + EPS) * g_ref[...]
    xb = xn.astype(BF16)

    def proj(lo, hi):
        return jnp.dot(xb, w_ref[:, lo:hi], preferred_element_type=F32)

    def put(ref, val, scale=None):
        if scale is not None:
            val = val * scale
        if blocked_t:
            for n in range(ref.shape[0]):
                ref[n] = val[n * ATT_BLOCK:(n + 1) * ATT_BLOCK].T.astype(BF16)
        else:
            ref[...] = val.astype(BF16)

    put(q_ref, proj(_C_Q, _C_K), DA_HEAD_DIM ** -0.5 * (LOG2E if blocked_t else 1.0))
    k = proj(_C_K, _C_V)
    kf_ref[...] = k
    kb_ref[...] = k.astype(BF16)
    v = proj(_C_V, _C_MC)
    vf_ref[...] = v
    put(vb_ref, v)
    mc_ref[...] = proj(_C_MC, _C_MV)
    mv_ref[...] = proj(_C_MV, _C_MO)
    mo_ref[...] = proj(_C_MO, _C_CU)
    cu_ref[...] = proj(_C_CU, _C_CV)
    cv_ref[...] = proj(_C_CV, _C_G)
    gc_ref[...] = proj(_C_G, IN_PAD)


def _in_proj(x2d, g, w_pad, tm, blocked_t):
    T = x2d.shape[0]
    row = lambda n: pl.BlockSpec((tm, n), lambda i: (i, 0))
    full = lambda a: pl.BlockSpec(a.shape, lambda i: (0,) * a.ndim)
    if blocked_t:
        tspec = pl.BlockSpec((tm // ATT_BLOCK, DA_WIDTH, ATT_BLOCK), lambda i: (i, 0, 0))
        tshape = jax.ShapeDtypeStruct((T // ATT_BLOCK, DA_WIDTH, ATT_BLOCK), BF16)
    else:
        tspec, tshape = row(DA_WIDTH), jax.ShapeDtypeStruct((T, DA_WIDTH), BF16)
    outs = [(DA_WIDTH, F32), (DA_WIDTH, F32), (DA_WIDTH, BF16), None,
            (ML_PAD, F32), (ML_PAD, F32), (ML_PAD, F32), (CM_WIDTH, F32), (CM_WIDTH, F32), (LANES, F32)]
    specs = [tspec] + [tspec if o is None else row(o[0]) for o in outs]
    shapes = [tshape] + [tshape if o is None else jax.ShapeDtypeStruct((T, o[0]), o[1]) for o in outs]
    return pl.pallas_call(
        functools.partial(_inproj_kernel, blocked_t=blocked_t),
        grid=(T // tm,),
        in_specs=[row(D_MODEL), full(g), full(w_pad)],
        out_specs=specs,
        out_shape=shapes,
        compiler_params=_cparams(("parallel",)),
        name="in_proj",
    )(x2d, g, w_pad)


def _rel_bucket_np(rel):
    half = REL_BUCKETS // 2
    max_exact = half // 2
    ret = np.where(rel > 0, half, 0)
    n = np.abs(rel)
    nf = np.maximum(n, 1).astype(np.float32)
    large = max_exact + (np.log(nf / np.float32(max_exact)) / np.float32(math.log(REL_MAX_DIST / max_exact))
                         * np.float32(half - max_exact)).astype(np.int32)
    large = np.minimum(large, half - 1)
    return (ret + np.where(n < max_exact, n, large)).astype(np.int32)


def _bias_kernel(tab_ref, idx_ref, o_ref, *, shift_bucket, scale):
    for n in range(idx_ref.shape[0]):
        idx = idx_ref[n]
        for h in range(DA_HEADS):
            acc = jnp.zeros(idx.shape, F32)
            for b in range(REL_BUCKETS):
                acc = jnp.where(idx == b, tab_ref[b, h], acc)
            if shift_bucket is not None:
                acc = acc - tab_ref[shift_bucket, h]
            o_ref[n, h] = acc * scale


def _bias_tiles(table, idx_np, shift_bucket, scale=1.0):
    n, r, c = idx_np.shape
    return pl.pallas_call(
        functools.partial(_bias_kernel, shift_bucket=shift_bucket, scale=scale),
        in_specs=[pl.BlockSpec(memory_space=pltpu.SMEM), pl.BlockSpec(memory_space=pltpu.VMEM)],
        out_specs=pl.BlockSpec(memory_space=pltpu.VMEM),
        out_shape=jax.ShapeDtypeStruct((n, DA_HEADS, r, c), F32),
        name="rel_bias",
    )(table, jnp.asarray(idx_np))


def _lambda(lam_ref, lam_init):
    lp = lam_ref[...]
    a = jnp.sum(lp[0:1] * lp[1:2], axis=-1, keepdims=True)
    b = jnp.sum(lp[2:3] * lp[3:4], axis=-1, keepdims=True)
    return jnp.exp(a) - jnp.exp(b) + lam_init


def _subln(o1, l1, o2, l2, lam, g, lam_init):
    o = o1 / l1 - lam * (o2 / l2)
    o = o * lax.rsqrt(jnp.mean(o * o, axis=-1, keepdims=True) + EPS) * g
    return o * (1.0 - lam_init)


def _attn_prompt_kernel(lam_ref, qt_ref, k_ref, vt_ref, bias_ref, g_ref, o_ref,
                        m_sc, l_sc, a_sc, acc_sc, s_sc, p_sc, *, lam_init):
    tb = ATT_BLOCK
    qb = pl.program_id(1)
    lam = _lambda(lam_ref, lam_init)
    krow = lax.broadcasted_iota(jnp.int32, (tb, tb), 0)
    qcol = lax.broadcasted_iota(jnp.int32, (tb, tb), 1)
    diag_mask = (krow // CHUNK) <= (qcol // CHUNK)
    comp0 = lax.broadcasted_iota(jnp.int32, (LANES, 1), 0) < DA_HEAD_DIM
    has_prev = qb >= 1

    m_sc[...] = jnp.full(m_sc.shape, NEG_INF, F32)
    l_sc[...] = jnp.zeros(l_sc.shape, F32)
    acc_sc[...] = jnp.zeros(acc_sc.shape, F32)

    qts = []
    for h in range(DA_HEADS):
        qt = qt_ref[0, LANES * h:LANES * (h + 1), :]
        qts.append((jnp.where(comp0, qt, jnp.zeros_like(qt)), jnp.where(comp0, jnp.zeros_like(qt), qt)))

    def sweep(kblk, bias_n=None, mask=None, nk=1):
        rk = slice(0, nk * tb)
        for h in range(DA_HEADS):
            kh = k_ref[0, pl.ds(pl.multiple_of(kblk * tb, tb), nk * tb), LANES * h:LANES * (h + 1)]
            for c in range(2):
                s = jnp.dot(kh, qts[h][c], preferred_element_type=F32)
                if bias_n is not None:
                    s = s + bias_ref[bias_n, h]
                if mask is not None:
                    s = jnp.where(mask, s, NEG_INF)
                s_sc[2 * h + c, rk] = s
        for ch in range(2 * DA_HEADS):
            s = s_sc[ch, rk]
            m_old = m_sc[ch]
            m_new = jnp.maximum(m_old, jnp.max(s, axis=0, keepdims=True))
            alpha = jnp.exp2(m_old - m_new)
            p = jnp.exp2(s - m_new[0:1])
            l_sc[ch] = alpha * l_sc[ch] + jnp.sum(p, axis=0, keepdims=True)
            m_sc[ch] = m_new
            a_sc[ch] = alpha
            p_sc[ch, rk] = p.astype(BF16)
        for h in range(DA_HEADS):
            vth = [vt_ref[kblk + n, LANES * h:LANES * (h + 1), :] for n in range(nk)]
            vth = vth[0] if nk == 1 else jnp.concatenate(vth, axis=1)
            for c in range(2):
                ch = 2 * h + c
                acc_sc[ch] = a_sc[ch][0:1] * acc_sc[ch] + jnp.dot(vth, p_sc[ch, rk], preferred_element_type=F32)

    sweep(qb, 0, diag_mask)
    sweep(jnp.maximum(qb - 1, 0), 1, has_prev)
    nfar = jnp.maximum(qb - 1, 0)
    lax.fori_loop(0, nfar // FAR_BLOCKS, lambda j, _: (sweep(FAR_BLOCKS * j, nk=FAR_BLOCKS), 0)[1], 0)
    lax.fori_loop(nfar // FAR_BLOCKS * FAR_BLOCKS, nfar, lambda j, _: (sweep(j), 0)[1], 0)

    for h in range(DA_HEADS):
        o = acc_sc[2 * h] / l_sc[2 * h][0:1] - lam * (acc_sc[2 * h + 1] / l_sc[2 * h + 1][0:1])
        o = o * lax.rsqrt(jnp.mean(o * o, axis=0, keepdims=True) + EPS) * g_ref[...]
        o_ref[0, :, LANES * h:LANES * (h + 1)] = (o * (1.0 - lam_init)).T.astype(o_ref.dtype)


def _attn_prompt(qt, k, vt, bias, lam_p, g_t, lam_init, B):
    S = k.shape[1]
    tb = ATT_BLOCK
    nb = S // tb
    return pl.pallas_call(
        functools.partial(_attn_prompt_kernel, lam_init=lam_init),
        grid=(B, nb),
        in_specs=[
            pl.BlockSpec(lam_p.shape, lambda b, i: (0, 0)),
            pl.BlockSpec((1, DA_WIDTH, tb), lambda b, i: (b * nb + i, 0, 0)),
            pl.BlockSpec((1, S, DA_WIDTH), lambda b, i: (b, 0, 0)),
            pl.BlockSpec((nb, DA_WIDTH, tb), lambda b, i: (b, 0, 0)),
            pl.BlockSpec(bias.shape, lambda b, i: (0, 0, 0, 0)),
            pl.BlockSpec(g_t.shape, lambda b, i: (0, 0)),
        ],
        out_specs=pl.BlockSpec((1, tb, DA_WIDTH), lambda b, i: (b, i, 0)),
        out_shape=jax.ShapeDtypeStruct((B, S, DA_WIDTH), BF16),
        scratch_shapes=[pltpu.VMEM((2 * DA_HEADS, SUBLANES, tb), F32),
                        pltpu.VMEM((2 * DA_HEADS, SUBLANES, tb), F32),
                        pltpu.VMEM((2 * DA_HEADS, SUBLANES, tb), F32),
                        pltpu.VMEM((2 * DA_HEADS, LANES, tb), F32),
                        pltpu.VMEM((2 * DA_HEADS, FAR_BLOCKS * tb, tb), F32),
                        pltpu.VMEM((2 * DA_HEADS, FAR_BLOCKS * tb, tb), BF16)],
        compiler_params=_cparams(("parallel", "arbitrary")),
        name="attn_prompt",
    )(lam_p, qt, k, vt, bias, g_t)


def _attn_step_kernel(lam_ref, q_ref, kn_ref, vn_ref, kp_ref, vp_ref, bp_ref, bn_ref, g_ref, o_ref, *, lam_init):
    lam = _lambda(lam_ref, lam_init)
    first = lax.broadcasted_iota(jnp.int32, (1, LANES), 1) < DA_HEAD_DIM
    g = g_ref[...]
    for h in range(DA_HEADS):
        sl = slice(LANES * h, LANES * (h + 1))
        q = q_ref[0, :, sl]
        kn = kn_ref[0, :, sl]
        vn = vn_ref[0, :, sl]
        kp = kp_ref[0, :, sl].astype(BF16)
        vp = vp_ref[0, :, sl].astype(BF16)
        res = []
        for c in range(2):
            qc = jnp.where(first if c == 0 else jnp.logical_not(first), q, jnp.zeros_like(q))
            sp = _dot_nt(qc, kp) + bp_ref[0, h]
            sn = _dot_nt(qc, kn) + bn_ref[0, h]
            m = jnp.maximum(jnp.max(sp, axis=-1, keepdims=True), jnp.max(sn, axis=-1, keepdims=True))
            pp = jnp.exp(sp - m)
            pn = jnp.exp(sn - m)
            l = jnp.sum(pp, axis=-1, keepdims=True) + jnp.sum(pn, axis=-1, keepdims=True)
            acc = (jnp.dot(pp.astype(BF16), vp, preferred_element_type=F32)
                   + jnp.dot(pn.astype(BF16), vn, preferred_element_type=F32))
            res.append((l, acc))
        (l1, a1), (l2, a2) = res
        o_ref[0, :, sl] = _subln(a1, l1, a2, l2, lam, g, lam_init).astype(o_ref.dtype)


def _attn_step(q, kn, vn, kp, vp, bias_p, bias_n, lam_p, g, lam_init):
    B, L, _ = q.shape
    P = kp.shape[1]
    c3 = lambda b: (b, 0, 0)
    z4 = lambda b: (0, 0, 0, 0)
    return pl.pallas_call(
        functools.partial(_attn_step_kernel, lam_init=lam_init),
        grid=(B,),
        in_specs=[
            pl.BlockSpec(lam_p.shape, lambda b: (0, 0)),
            pl.BlockSpec((1, L, DA_WIDTH), c3), pl.BlockSpec((1, L, DA_WIDTH), c3), pl.BlockSpec((1, L, DA_WIDTH), c3),
            pl.BlockSpec((1, P, DA_WIDTH), c3), pl.BlockSpec((1, P, DA_WIDTH), c3),
            pl.BlockSpec(bias_p.shape, z4), pl.BlockSpec(bias_n.shape, z4),
            pl.BlockSpec(g.shape, lambda b: (0, 0)),
        ],
        out_specs=pl.BlockSpec((1, L, DA_WIDTH), c3),
        out_shape=jax.ShapeDtypeStruct((B, L, DA_WIDTH), BF16),
        compiler_params=_cparams(("parallel",)),
        name="attn_step",
    )(lam_p, q, kn, vn, kp, vp, bias_p, bias_n, g)


def _log_sigmoid(x):
    return jnp.minimum(x, 0.0) - jnp.log(1.0 + jnp.exp(-jnp.abs(x)))


def _mlstm_kernel(gb_ref, mc_ref, mv_ref, mo_ref, gc_ref, cw_ref, cb_ref, wq_ref, wk_ref, ng_ref, sk_ref,
                  c0_ref, n0_ref, m0_ref, cv0_ref,
                  o_ref, co_ref, no_ref, mo2_ref, cvo_ref,
                  xbuf, c_sc, n_sc, m_sc):
    j = pl.program_id(1)
    L = CHUNK
    bb = mc_ref.shape[0]
    chains = [(b, h) for b in range(bb) for h in range(ML_HEADS)]
    hsl = lambda h: slice(LANES * h, LANES * (h + 1))

    @pl.when(j == 0)
    def _():
        xbuf[:, 5:8, :] = cv0_ref[...]
        c_sc[...] = c0_ref[...]
        n_sc[...] = n0_ref[...]
        m_sc[...] = m0_ref[...]

    r = lax.broadcasted_iota(jnp.int32, (L, L), 0)
    s = lax.broadcasted_iota(jnp.int32, (L, L), 1)
    tril = s <= r

    cc, gcol, gt = [], [], []
    for b in range(bb):
        x = mc_ref[b]
        xbuf[b, 8:8 + L, :] = x
        y = (cb_ref[...] + cw_ref[3:4, :] * x + cw_ref[2:3, :] * xbuf[b, 7:7 + L, :]
             + cw_ref[1:2, :] * xbuf[b, 6:6 + L, :] + cw_ref[0:1, :] * xbuf[b, 5:5 + L, :])
        cc.append(y * jax.nn.sigmoid(y))
        tail = xbuf[b, 5 + L:8 + L, :]
        xbuf[b, 5:8, :] = tail
        cvo_ref[b] = tail
        g = gc_ref[b]
        gcol.append(g)
        gt.append(jnp.concatenate([g, jnp.zeros((LANES - L, LANES), F32)], axis=0).T)

    qs, ks = {}, {}
    for b, h in chains:
        cch = cc[b][:, hsl(h)]
        qs[b, h] = jnp.dot(cch, wq_ref[h], preferred_element_type=F32)
        ks[b, h] = jnp.dot(cch, wk_ref[h], preferred_element_type=F32) * (ML_HEAD_DIM ** -0.5)
    sc, rd = {}, {}
    for b, h in chains:
        sc[b, h] = _dot_nt(qs[b, h], ks[b, h])
        rd[b, h] = jnp.dot(qs[b, h], c_sc[b, h], preferred_element_type=F32)
    ig_c, ig_r, lf_c, lf_r = {}, {}, {}, {}
    for b, h in chains:
        b_i = gb_ref[0, h]
        b_f = gb_ref[1, h]
        ig_c[b, h] = gcol[b][:, h:h + 1] + b_i
        lf_c[b, h] = _log_sigmoid(gcol[b][:, 4 + h:5 + h] + b_f)
        ig_r[b, h] = gt[b][h:h + 1, 0:L] + b_i
        lf_r[b, h] = _log_sigmoid(gt[b][4 + h:5 + h, 0:L] + b_f)
    f_c, f_r = {}, {}
    for ch in chains:
        f_c[ch] = jnp.sum(jnp.where(tril, lf_r[ch], 0.0), axis=1, keepdims=True)
        f_r[ch] = jnp.sum(jnp.where(r <= s, lf_c[ch], 0.0), axis=0, keepdims=True)
    dmat, dmax, tmax, m_prev, fl = {}, {}, {}, {}, {}
    for b, h in chains:
        ch = (b, h)
        m_prev[ch] = m_sc[b, h][0:1, 0:1]
        fl[ch] = f_c[ch][L - 1:L, :]
        dmat[ch] = jnp.where(tril, f_c[ch] - f_r[ch] + ig_r[ch], NEG_INF)
        dmax[ch] = jnp.max(dmat[ch], axis=1, keepdims=True)
        tmax[ch] = jnp.max(fl[ch] - f_r[ch] + ig_r[ch], axis=1, keepdims=True)
    sw, iw, m_t, kw, wc, m_new = {}, {}, {}, {}, {}, {}
    for ch in chains:
        inter = f_c[ch] + m_prev[ch]
        m_t[ch] = jnp.maximum(inter, dmax[ch])
        sw[ch] = sc[ch] * jnp.exp(dmat[ch] - m_t[ch])
        iw[ch] = jnp.exp(inter - m_t[ch])
        m_new[ch] = jnp.maximum(fl[ch] + m_prev[ch], tmax[ch])
        wc[ch] = jnp.exp(fl[ch] + m_prev[ch] - m_new[ch])
        kw[ch] = ks[ch] * jnp.exp(fl[ch] - f_c[ch] + ig_c[ch] - m_new[ch])
    nv, up = {}, {}
    for b, h in chains:
        vh = mv_ref[b, :, hsl(h)]
        nv[b, h] = jnp.dot(sw[b, h], vh, preferred_element_type=F32)
        up[b, h] = _dot_tn(kw[b, h], vh)
    ones_l = jnp.ones((L, LANES), F32)
    ones_w = jnp.ones((LANES, LANES), F32)
    rowsum = lambda x, ones: jnp.dot(x, ones, preferred_element_type=F32, precision=lax.Precision.HIGHEST)
    n_prev, ksum, iw_b, emt_b = {}, {}, {}, {}
    for b, h in chains:
        ch = (b, h)
        n_prev[ch] = n_sc[b, h]
        ksum[ch] = jnp.sum(kw[ch], axis=0, keepdims=True)
        iw_b[ch] = jnp.broadcast_to(iw[ch], (L, LANES))
        emt_b[ch] = jnp.broadcast_to(jnp.exp(-m_t[ch]), (L, LANES))
    den = {}
    for ch in chains:
        den[ch] = rowsum(sw[ch], ones_l) + iw_b[ch] * rowsum(qs[ch] * n_prev[ch], ones_w)
    hout, ssq = {}, {}
    for ch in chains:
        hout[ch] = (nv[ch] + iw_b[ch] * rd[ch]) / jnp.maximum(jnp.abs(den[ch]), emt_b[ch])
    for ch in chains:
        ssq[ch] = rowsum(hout[ch] * hout[ch], ones_w)
    for b, h in chains:
        ch = (b, h)
        c_new = wc[ch] * c_sc[b, h] + up[ch]
        n_new = wc[ch] * n_prev[ch] + ksum[ch]
        m_row = jnp.broadcast_to(m_new[ch], (1, LANES))
        c_sc[b, h] = c_new
        n_sc[b, h] = n_new
        m_sc[b, h] = m_row
        co_ref[b, h] = c_new
        no_ref[b, h] = n_new
        mo2_ref[b, h] = m_row
        hn = hout[ch] * lax.rsqrt(ssq[ch] * (1.0 / ML_HEAD_DIM) + EPS) * ng_ref[:, hsl(h)]
        o_ref[b, :, hsl(h)] = ((hn + sk_ref[:, hsl(h)] * cc[b][:, hsl(h)])
                               * jax.nn.sigmoid(mo_ref[b, :, hsl(h)])).astype(o_ref.dtype)


def _mlstm(gate_b, mc, mv, mo, gc, cw, cb, wq, wk, ng, sk, c0, n0, m0, cv0, bb):
    B, L, _ = mc.shape
    nc = L // CHUNK
    tok = lambda n: pl.BlockSpec((bb, CHUNK, n), lambda b, j: (b, j, 0))
    full = lambda a: pl.BlockSpec(a.shape, lambda b, j: (0,) * a.ndim)
    st4 = lambda a: pl.BlockSpec((bb,) + a.shape[1:], lambda b, j: (b,) + (0,) * (a.ndim - 1))
    return pl.pallas_call(
        _mlstm_kernel,
        grid=(B // bb, nc),
        in_specs=[pl.BlockSpec(memory_space=pltpu.SMEM),
                  tok(ML_PAD), tok(ML_PAD), tok(ML_PAD), tok(LANES),
                  full(cw), full(cb), full(wq), full(wk), full(ng), full(sk),
                  st4(c0), st4(n0), st4(m0), st4(cv0)],
        out_specs=[tok(ML_PAD), st4(c0), st4(n0), st4(m0), st4(cv0)],
        out_shape=[jax.ShapeDtypeStruct((B, L, ML_PAD), BF16),
                   jax.ShapeDtypeStruct(c0.shape, F32), jax.ShapeDtypeStruct(n0.shape, F32),
                   jax.ShapeDtypeStruct(m0.shape, F32), jax.ShapeDtypeStruct(cv0.shape, F32)],
        scratch_shapes=[pltpu.VMEM((bb, 8 + CHUNK, ML_PAD), F32),
                        pltpu.VMEM((bb, ML_HEADS, LANES, LANES), F32),
                        pltpu.VMEM((bb, ML_HEADS, 1, LANES), F32),
                        pltpu.VMEM((bb, ML_HEADS, 1, LANES), F32)],
        compiler_params=_cparams(("parallel", "arbitrary")),
        name="mlstm",
    )(gate_b, mc, mv, mo, gc, cw, cb, wq, wk, ng, sk, c0, n0, m0, cv0)


def _mixout_kernel(x_ref, oda_ref, oml_ref, cu_ref, cv_ref, cmg_ref, ws_ref, bm_ref,
                   wda_ref, wml_ref, wcm_ref, x1_ref, vcm_ref, *, lc):
    tm = x_ref.shape[0]
    u = _gelu(cu_ref[...])
    gv = _gelu(cv_ref[...])
    vcm = gv * lax.rsqrt(jnp.mean(gv * gv, axis=-1, keepdims=True) + EPS) * cmg_ref[...]
    vcm_ref[...] = vcm
    r = lax.broadcasted_iota(jnp.int32, (lc, lc), 0)
    c = lax.broadcasted_iota(jnp.int32, (lc, lc), 1)
    grp = lax.broadcasted_iota(jnp.int32, (1, CM_WIDTH), 1) // CM_GROUP_DIM
    wsm = [jnp.where(c <= r, ws_ref[g, 0:lc, 0:lc], 0.0).astype(BF16) for g in range(CM_GROUPS)]
    pieces = []
    for ci in range(tm // lc):
        vch = vcm[ci * lc:(ci + 1) * lc].astype(BF16)
        mixed = bm_ref[0:lc, :]
        for g in range(CM_GROUPS):
            mixed = mixed + jnp.where(grp == g, jnp.dot(wsm[g], vch, preferred_element_type=F32), 0.0)
        pieces.append(u[ci * lc:(ci + 1) * lc] * mixed)
    ocm = jnp.concatenate(pieces, axis=0) if len(pieces) > 1 else pieces[0]
    y = (jnp.dot(oda_ref[...], wda_ref[...], preferred_element_type=F32)
         + jnp.dot(oml_ref[...], wml_ref[...], preferred_element_type=F32)
         + jnp.dot(ocm.astype(BF16), wcm_ref[...], preferred_element_type=F32))
    x1_ref[...] = x_ref[...] + y


def _mix_out(x2d, oda, oml, cu, cv, cmg, ws, bmat, wda, wml, wcm, tm, lc):
    T = x2d.shape[0]
    row = lambda n: pl.BlockSpec((tm, n), lambda i: (i, 0))
    full = lambda a: pl.BlockSpec(a.shape, lambda i: (0,) * a.ndim)
    return pl.pallas_call(
        functools.partial(_mixout_kernel, lc=lc),
        grid=(T // tm,),
        in_specs=[row(D_MODEL), row(DA_WIDTH), row(ML_PAD), row(CM_WIDTH), row(CM_WIDTH),
                  full(cmg), full(ws), full(bmat), full(wda), full(wml), full(wcm)],
        out_specs=[row(D_MODEL), row(CM_WIDTH)],
        out_shape=[jax.ShapeDtypeStruct((T, D_MODEL), F32), jax.ShapeDtypeStruct((T, CM_WIDTH), F32)],
        compiler_params=_cparams(("parallel",)),
        name="mix_out",
    )(x2d, oda, oml, cu, cv, cmg, ws, bmat, wda, wml, wcm)


_NTOP = PEER_TOPK + 1
_PAIRS = [(a, b) for a in range(_NTOP) for b in range(_NTOP) if (a + 1) * (b + 1) <= _NTOP]
_NCAND = ((len(_PAIRS) + 7) // 8) * 8


def _merge_exchange_network(n):
    pairs = []
    t = max(1, math.ceil(math.log2(n)))
    p = 1 << (t - 1)
    while p > 0:
        q, r, d = 1 << (t - 1), 0, p
        while d > 0:
            pairs += [(i, i + d) for i in range(n - d) if (i & p) == r]
            d, q, r = q - p, q >> 1, p
        p >>= 1
    return pairs


_KEY_SLABS = PEER_KEYS // SUBLANES
_SORT_NET = _merge_exchange_network(_KEY_SLABS)


def _peer_select_lanes(st_ref, a_ref, b_ref, bd_ref, vals, cand, ls):
    tt = ls.stop - ls.start
    for h in range(PEER_HEADS):
        for c in range(2):
            v = [st_ref[h, c, SUBLANES * j:SUBLANES * (j + 1), ls] for j in range(_KEY_SLABS)]
            for i, j in _SORT_NET:
                v[i], v[j] = jnp.maximum(v[i], v[j]), jnp.minimum(v[i], v[j])
            for i in range(_NTOP):
                m = jnp.max(v[0], axis=0, keepdims=True)
                vals[c, i:i + 1, :] = m
                keep = _NTOP - 1 - i
                popped = v[0] == m
                for j in range(min(keep, _KEY_SLABS)):
                    v[j] = jnp.where(popped, v[j + 1] if j + 1 < _KEY_SLABS else NEG_INF, v[j])
        cand[...] = jnp.full((_NCAND, tt), NEG_INF, F32)
        for i, (a, b) in enumerate(_PAIRS):
            cand[i:i + 1, :] = vals[0, a:a + 1, :] + vals[1, b:b + 1, :]
        top1 = vals[0, 0:1, :]
        top2 = vals[1, 0:1, :]
        best = top1 + top2
        cur = cand[...]
        z = jnp.zeros((1, tt), F32)
        c16 = best
        for i in range(PEER_TOPK):
            c16 = jnp.max(cur, axis=0, keepdims=True)
            z = z + jnp.exp(c16 - best)
            cur = jnp.where(cur == c16, NEG_INF, cur)
        thr = 0.5 * (c16 + jnp.max(cur, axis=0, keepdims=True))
        s1 = st_ref[h, 0, :, ls]
        a_ref[h, :, ls] = 0.5 * jnp.exp(s1 - top1) / z
        b_ref[h, :, ls] = jnp.exp(st_ref[h, 1, :, ls] - top2)
        bd_ref[h, :, ls] = jnp.exp(thr - s1 - top2)


def _peer_route_kernel(x_ref, g_ref, wq_ref, keys_ref, xnt_ref, a_ref, b_ref, bd_ref, st_sc, vals, cand):
    x = x_ref[...]
    xf = x * lax.rsqrt(jnp.mean(x * x, axis=-1, keepdims=True) + EPS) * g_ref[...]
    xn = xf.astype(BF16)
    xnt_ref[...] = xf.T.astype(BF16)
    qs = [jnp.dot(xn, wq_ref[:, PEER_QDIM * h:PEER_QDIM * (h + 1)], preferred_element_type=F32)
          for h in range(PEER_HEADS)]
    qs = [(q * lax.rsqrt(jnp.mean(q * q, axis=-1, keepdims=True) + EPS)).astype(BF16) for q in qs]
    for h in range(PEER_HEADS):
        for c in range(2):
            qc = qs[h][:, PEER_HALF * c:PEER_HALF * (c + 1)]
            st_sc[h, c] = _dot_nt(keys_ref[h, c], qc)
    for lt in range(x_ref.shape[0] // LANES):
        _peer_select_lanes(st_sc, a_ref, b_ref, bd_ref, vals, cand, slice(lt * LANES, (lt + 1) * LANES))


def _peer_route(x1, g, wq, keys, tm):
    T = x1.shape[0]
    full = lambda a: pl.BlockSpec(a.shape, lambda i: (0,) * a.ndim)
    spec = pl.BlockSpec((PEER_HEADS, PEER_KEYS, tm), lambda i: (0, 0, i))
    shp = jax.ShapeDtypeStruct((PEER_HEADS, PEER_KEYS, T), F32)
    return pl.pallas_call(
        _peer_route_kernel,
        grid=(T // tm,),
        in_specs=[pl.BlockSpec((tm, D_MODEL), lambda i: (i, 0)), full(g), full(wq), full(keys)],
        out_specs=[pl.BlockSpec((D_MODEL, tm), lambda i: (0, i)), spec, spec, spec],
        out_shape=[jax.ShapeDtypeStruct((D_MODEL, T), BF16), shp, shp, shp],
        scratch_shapes=[pltpu.VMEM((PEER_HEADS, 2, PEER_KEYS, tm), F32),
                        pltpu.VMEM((2, 3 * SUBLANES, LANES), F32), pltpu.VMEM((_NCAND, LANES), F32)],
        compiler_params=_cparams(("parallel",)),
        name="peer_route",
    )(x1, g, wq, keys)


def _peer_dense_kernel(x1_ref, xnt_ref, a_ref, b_ref, bd_ref, u_ref, vt_ref, fg_ref, o_ref,
                       acc_ref, *scr, rows, final_norm):
    nchunk = len(scr) // 2
    ht_refs, at_refs = scr[:nchunk], scr[nchunk:]
    j = pl.program_id(1)
    last = pl.num_programs(1) - 1

    @pl.when(j == 0)
    def _():
        acc_ref[...] = jnp.zeros_like(acc_ref)

    ch, tt = ht_refs[0].shape
    per = PEER_ROWS_PER_SLAB
    assert nchunk * ch == rows * PEER_KEYS
    lw = LANES
    assert SUBLANES % per == 0 and (ch // PEER_KEYS) % per == 0

    def proj(c):
        cs = slice(c * ch, (c + 1) * ch)
        ht_refs[c][...] = jnp.dot(u_ref[cs, :], xnt_ref[...], preferred_element_type=F32)

    def gate(c):
        for p in range(ch // PEER_KEYS // per):
            gate_rows(c, range(c * (ch // PEER_KEYS) + p * per, c * (ch // PEER_KEYS) + (p + 1) * per))

    def gate_rows(c, rr):
        heads = range(PEER_HEADS)
        for lt in range(tt // lw):
            ls = slice(lt * lw, (lt + 1) * lw)
            i0 = pl.multiple_of(j * rows + rr[0] // SUBLANES * SUBLANES, SUBLANES)
            sub = [r % SUBLANES for r in rr]
            att = [a_ref[h, pl.ds(i0, SUBLANES), ls] for h in heads]
            bdt = [bd_ref[h, pl.ds(i0, SUBLANES), ls] for h in heads]
            av = [[jnp.broadcast_to(att[h][q:q + 1], (SUBLANES, lw)) for h in heads] for q in sub]
            bdv = [[jnp.broadcast_to(bdt[h][q:q + 1], (SUBLANES, lw)) for h in heads] for q in sub]
            for sg in range(PEER_KEYS // (2 * SUBLANES)):
                halves = [[] for _ in rr]
                for half in range(2):
                    k0 = sg * 2 * SUBLANES + half * SUBLANES
                    bv = [b_ref[h, k0:k0 + SUBLANES, ls] for h in heads]
                    for ri, r in enumerate(rr):
                        t = [jnp.where(bv[h] >= bdv[ri][h], bv[h], 0.0) * av[ri][h] for h in heads]
                        while len(t) > 1:
                            t = [t[i] + t[i + 1] for i in range(0, len(t), 2)]
                        h0 = r * PEER_KEYS - c * ch + k0
                        ht = ht_refs[c][h0:h0 + SUBLANES, ls]
                        halves[ri].append(t[0] * (ht * (1.0 + lax.erf(ht * (2.0 ** -0.5)))))
                for ri, r in enumerate(rr):
                    o0 = r * PEER_KEYS - c * ch + sg * 2 * SUBLANES
                    at_refs[c][o0:o0 + 2 * SUBLANES, ls] = jnp.concatenate(halves[ri], axis=0).astype(BF16)

    def value(c):
        acc_ref[...] += jnp.dot(vt_ref[:, c * ch:(c + 1) * ch], at_refs[c][...], preferred_element_type=F32)

    proj(0)
    for c in range(nchunk):
        if c + 1 < nchunk:
            proj(c + 1)
        gate(c)
        value(c)

    @pl.when(j == last)
    def _():
        x2 = x1_ref[...] + acc_ref[...].T
        if final_norm:
            x2 = x2 * lax.rsqrt(jnp.mean(x2 * x2, axis=-1, keepdims=True) + EPS) * fg_ref[...]
        o_ref[...] = x2


def _peer_dense(x1, xn, a, b, bd, u_bf, vt_bf, fg, tt, eb, final_norm):
    T = x1.shape[0]
    rows = eb // PEER_KEYS
    ne = PEER_EXPERTS // eb
    hk = pl.BlockSpec((PEER_HEADS, PEER_KEYS, tt), lambda i, j: (0, 0, i))
    return pl.pallas_call(
        functools.partial(_peer_dense_kernel, rows=rows, final_norm=final_norm),
        grid=(T // tt, ne),
        in_specs=[pl.BlockSpec((tt, D_MODEL), lambda i, j: (i, 0)),
                  pl.BlockSpec((D_MODEL, tt), lambda i, j: (0, i)),
                  hk, hk, hk,
                  pl.BlockSpec((eb, D_MODEL), lambda i, j: (j, 0)),
                  pl.BlockSpec((D_MODEL, eb), lambda i, j: (0, j)),
                  pl.BlockSpec(fg.shape, lambda i, j: (0, 0))],
        out_specs=pl.BlockSpec((tt, D_MODEL), lambda i, j: (i, 0)),
        out_shape=jax.ShapeDtypeStruct((T, D_MODEL), F32),
        scratch_shapes=([pltpu.VMEM((D_MODEL, tt), F32)]
                        + [pltpu.VMEM((PEER_CHUNK_EXPERTS, tt), F32)] * (eb // PEER_CHUNK_EXPERTS)
                        + [pltpu.VMEM((PEER_CHUNK_EXPERTS, tt), BF16)] * (eb // PEER_CHUNK_EXPERTS)),
        compiler_params=_cparams(("parallel", "arbitrary")),
        name="peer_dense",
    )(x1, xn, a, b, bd, u_bf, vt_bf, fg)


def _pad_heads(w, axis=-1):
    w = jnp.moveaxis(w, axis, -1)
    lead = w.shape[:-1]
    w = w.reshape(lead + (ML_HEADS, ML_HEAD_DIM))
    w = jnp.pad(w, [(0, 0)] * len(lead) + [(0, 0), (0, LANES - ML_HEAD_DIM)])
    return jnp.moveaxis(w.reshape(lead + (ML_PAD,)), -1, axis)


def _layer_params(l, norm1_g, w_in, da_lambda, da_subln_g, ml_conv_w, ml_conv_b, ml_wq, ml_wk, ml_gate_b,
                  ml_norm_g, ml_skip, cm_norm_g, cm_ws, cm_b, w_out, norm2_g, peer_wq, peer_keys, peer_u, peer_v):
    o = np.cumsum((0, 512, 512, 512, 256, 256, 256, 4, 4, 256, 256))
    seg = lambda i: w_in[:, o[i]:o[i + 1]]
    gates = jnp.pad(jnp.concatenate([seg(6), seg(7)], axis=1), ((0, 0), (0, LANES - 2 * ML_HEADS)))
    w_pad = jnp.concatenate([seg(0), seg(1), seg(2), _pad_heads(seg(3)), _pad_heads(seg(4)), _pad_heads(seg(5)),
                             seg(8), seg(9), gates], axis=1).astype(BF16)
    pad_sq = lambda w: jnp.pad(w, ((0, 0), (0, LANES - ML_HEAD_DIM), (0, LANES - ML_HEAD_DIM)))
    return dict(
        lam_init=0.8 - 0.6 * math.exp(-0.3 * l),
        g1=norm1_g.reshape(1, D_MODEL), w_pad=w_pad,
        lam_p=da_lambda, subln_g=da_subln_g.reshape(1, DA_VDIM),
        cw=_pad_heads(ml_conv_w), cb=_pad_heads(ml_conv_b.reshape(1, ML_WIDTH)),
        wq=pad_sq(ml_wq), wk=pad_sq(ml_wk), gate_b=ml_gate_b,
        ng=_pad_heads(ml_norm_g.reshape(1, ML_WIDTH)), sk=_pad_heads(ml_skip.reshape(1, ML_WIDTH)),
        cmg=cm_norm_g.reshape(1, CM_WIDTH), ws=cm_ws,
        bmat=jnp.repeat(cm_b.T, CM_GROUP_DIM, axis=1),
        wda=w_out[:DA_WIDTH].astype(BF16),
        wml=_pad_heads(w_out[DA_WIDTH:DA_WIDTH + ML_WIDTH], axis=0).astype(BF16),
        wcm=w_out[DA_WIDTH + ML_WIDTH:].astype(BF16),
        g2=norm2_g.reshape(1, D_MODEL), pwq=peer_wq.astype(BF16), pkeys=peer_keys.astype(BF16),
        u_bf=peer_u.astype(BF16), vt_bf=peer_v.T.astype(BF16),
    )


def _pick(n, prefs):
    for p in prefs:
        if n % p == 0:
            return p
    return n


def _layer(x, past, p, bias, final_g, final_norm):
    B, L, _ = x.shape
    T = B * L
    x2d = x.reshape(T, D_MODEL)
    tm = _pick(T, (512, 256, 128))
    q, kf, vf, kb, vb, mc, mv, mo, cu, cv, gc = _in_proj(x2d, p["g1"], p["w_pad"], tm, past is None)
    r3 = lambda a: a.reshape(B, L, a.shape[-1])
    if past is None:
        g_t = jnp.broadcast_to(p["subln_g"].reshape(DA_VDIM, 1), (DA_VDIM, ATT_BLOCK))
        oda = _attn_prompt(q, r3(kb), vb, bias, p["lam_p"], g_t, p["lam_init"], B)
        c0 = jnp.zeros((B, ML_HEADS, LANES, LANES), F32)
        n0 = jnp.zeros((B, ML_HEADS, 1, LANES), F32)
        m0 = jnp.zeros((B, ML_HEADS, 1, LANES), F32)
        cv0 = jnp.zeros((B, ML_CONV - 1, ML_PAD), F32)
        lc = CM_CHUNK
    else:
        pk, pv, pc, pn, pm, pconv = past
        P = pk.shape[1]
        oda = _attn_step(r3(q), r3(kb), r3(vb), pk.reshape(B, P, DA_WIDTH), pv.reshape(B, P, DA_WIDTH),
                         bias[0], bias[1], p["lam_p"], p["subln_g"], p["lam_init"])
        c0 = jnp.pad(pc, ((0, 0), (0, 0), (0, LANES - ML_HEAD_DIM), (0, LANES - ML_HEAD_DIM)))
        n0 = jnp.pad(pn, ((0, 0), (0, 0), (0, LANES - ML_HEAD_DIM)))[:, :, None, :]
        m0 = jnp.broadcast_to(pm[:, :, None, None], (B, ML_HEADS, 1, LANES))
        cv0 = _pad_heads(pconv)
        lc = L
    oml, c_new, n_new, m_new, conv_new = _mlstm(
        p["gate_b"], r3(mc), r3(mv), r3(mo), r3(gc), p["cw"], p["cb"], p["wq"], p["wk"], p["ng"], p["sk"],
        c0, n0, m0, cv0, ML_SEQS_PER_STEP)
    x1, vcm = _mix_out(x2d, oda.reshape(T, DA_WIDTH), oml.reshape(T, ML_PAD), cu, cv, p["cmg"], p["ws"], p["bmat"],
                       p["wda"], p["wml"], p["wcm"], _pick(T, (512, 256, 128)), lc)
    xn2, a, b, bd = _peer_route(x1, p["g2"], p["pwq"], p["pkeys"], _pick(T, (256, 128)))
    x2 = _peer_dense(x1, xn2, a, b, bd, p["u_bf"], p["vt_bf"], final_g, _pick(T, (512, 256, 128)), PEER_BLOCK_EXPERTS, final_norm)
    state = (kf, vf,

             c_new[:, :, :ML_HEAD_DIM, :ML_HEAD_DIM], n_new[:, :, 0, :ML_HEAD_DIM], m_new[:, :, 0, 0],
             conv_new.reshape(B, ML_CONV - 1, ML_HEADS, LANES)[..., :ML_HEAD_DIM].reshape(B, ML_CONV - 1, ML_WIDTH),
             vcm.reshape(B, L, CM_WIDTH))
    return x2.reshape(B, L, D_MODEL), state


def kernel(x_prompt, x_sample, cache_k, cache_v, state_mlstm_c, state_mlstm_n, state_mlstm_m, state_mlstm_conv, norm1_g, w_in, da_lambda, da_subln_g, rel_bias_table, ml_conv_w, ml_conv_b, ml_wq, ml_wk, ml_gate_b, ml_norm_g, ml_skip, cm_norm_g, cm_ws, cm_b, w_out, norm2_g, peer_wq, peer_keys, peer_u, peer_v, final_g):
    depth = w_in.shape[0]
    S = x_prompt.shape[1]
    Ld = x_sample.shape[1]
    P = cache_k.shape[2]
    assert S % ATT_BLOCK == 0
    far_bucket = int(_rel_bucket_np(np.array(-(ATT_BLOCK + 1))))
    ii = np.arange(ATT_BLOCK)
    idx_prompt = np.stack([_rel_bucket_np(ii[:, None] - ii[None, :]),
                           _rel_bucket_np(ii[:, None] - ATT_BLOCK - ii[None, :])])
    bias_prompt = _bias_tiles(rel_bias_table, idx_prompt, far_bucket, LOG2E)
    qpos = P + np.arange(Ld)
    bias_past = _bias_tiles(rel_bias_table, _rel_bucket_np(np.arange(P)[None, :] - qpos[:, None])[None], None)
    bias_new = _bias_tiles(rel_bias_table, _rel_bucket_np(qpos[None, :] - qpos[:, None])[None], None)
    fg = final_g.reshape(1, D_MODEL)

    hp, hs = x_prompt, x_sample
    st_p, st_s = [], []
    for l in range(depth):
        p = _layer_params(l, norm1_g[l], w_in[l], da_lambda[l], da_subln_g[l], ml_conv_w[l], ml_conv_b[l],
                          ml_wq[l], ml_wk[l], ml_gate_b[l], ml_norm_g[l], ml_skip[l], cm_norm_g[l], cm_ws[l],
                          cm_b[l], w_out[l], norm2_g[l], peer_wq[l], peer_keys[l], peer_u[l], peer_v[l])
        last = l == depth - 1
        hp, sp = _layer(hp, None, p, bias_prompt, fg, last)
        past = (cache_k[l], cache_v[l], state_mlstm_c[l], state_mlstm_n[l], state_mlstm_m[l], state_mlstm_conv[l])
        hs, ss = _layer(hs, past, p, (bias_past, bias_new), fg, last)
        st_p.append(sp)
        st_s.append(ss)
    stack = lambda sts, i: jnp.stack([s[i] for s in sts])
    heads = lambda a, x: a.reshape(depth, x.shape[0], x.shape[1], DA_HEADS, DA_VDIM)
    return (hp, hs,
            heads(stack(st_p, 0), x_prompt), heads(stack(st_p, 1), x_prompt),
            stack(st_p, 2), stack(st_p, 3), stack(st_p, 4), stack(st_p, 5),
            heads(stack(st_s, 0), x_sample), heads(stack(st_s, 1), x_sample),
            stack(st_s, 2), stack(st_s, 3), stack(st_s, 4), stack(st_s, 5), stack(st_s, 6))
```
